```python
import math
import jax, jax.numpy as jnp
from jax import lax
import numpy as np

D_MODEL = 1024
BATCH = 4
SEQ = 8192
DEPTH = 4

N_MIXERS = 3
N_POOL_LAYERS = (DEPTH + 2) // 3
N_SSD_LAYERS = (DEPTH + 1) // 3
N_SB_LAYERS = DEPTH // 3
NORM_EPS = 1e-6

POOL_GROUPS = 4
POOL_WINDOWS = (2, 4, 8, 16)
POOL_GROUP_DIM = D_MODEL // POOL_GROUPS

SSD_D_INNER = 2 * D_MODEL
SSD_HEAD_DIM = 64
SSD_HEADS = SSD_D_INNER // SSD_HEAD_DIM
SSD_GROUPS = 8
SSD_HEADS_PER_GROUP = SSD_HEADS // SSD_GROUPS
SSD_STATE = 128
SSD_CONV = 4
SSD_CHUNK = 256
SSD_GN = SSD_GROUPS * SSD_STATE
SSD_CONV_CH = SSD_D_INNER + 2 * SSD_GN
SSD_IN_DIM = SSD_D_INNER + SSD_CONV_CH + SSD_HEADS
SSD_NORM_GROUP = SSD_D_INNER // SSD_GROUPS

SB_HEADS = 16
SB_HEAD_DIM = D_MODEL // SB_HEADS
SB_BLOCK = 128

FFN_HIDDEN = ((8 * D_MODEL + 3 * 256 - 1) // (3 * 256)) * 256

kernel_name = "hybrid_pool_ssd_stickbreak_block"


def rms_norm(x, gain):
    xf = x.astype(jnp.float32)
    y = xf * lax.rsqrt(jnp.mean(xf * xf, axis=-1, keepdims=True) + NORM_EPS)
    return (y * gain.astype(jnp.float32)).astype(x.dtype)


def pool_mixer(h, w_in, w_group, scale):
    b, s, _ = h.shape
    u = (h @ w_in).reshape(b, s, POOL_GROUPS, POOL_GROUP_DIM).astype(jnp.float32)
    cs = jnp.cumsum(u, axis=1)
    pos = jnp.arange(s)
    outs = []
    for g, w in enumerate(POOL_WINDOWS):
        csg = cs[:, :, g]
        lagged = jnp.pad(csg[:, : s - w], ((0, 0), (w, 0), (0, 0)))
        count = jnp.minimum(pos + 1, w).astype(jnp.float32)[None, :, None]
        outs.append((csg - lagged) / count - u[:, :, g])
    p = jnp.stack(outs, axis=2)
    y = jnp.einsum('bsgc,gcd->bsgd', p, w_group.astype(jnp.float32))
    y = y.reshape(b, s, D_MODEL) * scale.astype(jnp.float32)
    return y.astype(h.dtype)


def ssd_chunked_scan(xdt, da, bmat, cmat):
    b, s = da.shape[:2]
    pad = (-s) % SSD_CHUNK

    def chunks(t):
        t = jnp.pad(t, [(0, 0), (0, pad)] + [(0, 0)] * (t.ndim - 2))
        return jnp.swapaxes(t.reshape(b, -1, SSD_CHUNK, *t.shape[2:]), 0, 1)

    causal = jnp.tril(jnp.ones((SSD_CHUNK, SSD_CHUNK), bool))[None, :, :, None, None]

    def step(state, inp):
        xc, ac, bc, cc = inp
        acum = jnp.cumsum(ac, axis=1)
        diff = acum[:, :, None] - acum[:, None, :]
        decay = jnp.exp(jnp.where(causal, diff, -jnp.inf))
        cb = jnp.einsum('btgn,bsgn->btsg', cc, bc)
        y = jnp.einsum('btsg,btsgh,bsghp->btghp', cb, decay, xc)
        y = y + jnp.einsum('btgn,bghpn,btgh->btghp', cc, state, jnp.exp(acum))
        a_last = acum[:, -1]
        w = jnp.exp(a_last[:, None] - acum)
        state = state * jnp.exp(a_last)[..., None, None] + jnp.einsum('bsgn,bsgh,bsghp->bghpn', bc, w, xc)
        return state, y

    state0 = jnp.zeros((b, SSD_GROUPS, SSD_HEADS_PER_GROUP, SSD_HEAD_DIM, SSD_STATE), jnp.float32)
    _, ys = lax.scan(step, state0, (chunks(xdt), chunks(da), chunks(bmat), chunks(cmat)))
    ys = jnp.swapaxes(ys, 0, 1).reshape(b, -1, *ys.shape[3:])
    return ys[:, :s]


def ssd_mixer(h, w_in, conv_w, conv_b, dt_bias, a_log, d_skip, out_norm, w_out):
    b, s, _ = h.shape
    f32 = jnp.float32
    proj = h @ w_in
    z = proj[..., :SSD_D_INNER]
    xbc = proj[..., SSD_D_INNER:SSD_D_INNER + SSD_CONV_CH]
    dt = proj[..., SSD_D_INNER + SSD_CONV_CH:]
    xbc = lax.conv_general_dilated(
        xbc, conv_w[:, None, :].astype(xbc.dtype), window_strides=(1,),
        padding=[(SSD_CONV - 1, 0)], dimension_numbers=('NWC', 'WIO', 'NWC'),
        feature_group_count=SSD_CONV_CH)
    xbc = jax.nn.silu(xbc.astype(f32) + conv_b.astype(f32))
    xs = xbc[..., :SSD_D_INNER].reshape(b, s, SSD_GROUPS, SSD_HEADS_PER_GROUP, SSD_HEAD_DIM)
    bm = xbc[..., SSD_D_INNER:SSD_D_INNER + SSD_GN].reshape(b, s, SSD_GROUPS, SSD_STATE)
    cm = xbc[..., SSD_D_INNER + SSD_GN:].reshape(b, s, SSD_GROUPS, SSD_STATE)
    dt = jax.nn.softplus(dt.astype(f32) + dt_bias.astype(f32)).reshape(b, s, SSD_GROUPS, SSD_HEADS_PER_GROUP)
    a = -jnp.exp(a_log.astype(f32)).reshape(SSD_GROUPS, SSD_HEADS_PER_GROUP)
    y = ssd_chunked_scan(xs * dt[..., None], dt * a, bm, cm)
    y = y + d_skip.astype(f32).reshape(SSD_GROUPS, SSD_HEADS_PER_GROUP, 1) * xs
    g = (y.reshape(b, s, SSD_D_INNER) * jax.nn.silu(z.astype(f32))).reshape(b, s, SSD_GROUPS, SSD_NORM_GROUP)
    g = g * lax.rsqrt(jnp.mean(g * g, axis=-1, keepdims=True) + NORM_EPS)
    g = g.reshape(b, s, SSD_D_INNER) * out_norm.astype(f32)
    return g.astype(h.dtype) @ w_out


def stick_breaking_mixer(h, w_qkv, q_norm, k_norm, w_out):
    b, s, _ = h.shape
    f32 = jnp.float32
    qkv = (h @ w_qkv).reshape(b, s, 3, SB_HEADS, SB_HEAD_DIM)
    q = rms_norm(qkv[:, :, 0], q_norm).astype(f32).transpose(0, 2, 1, 3)
    k = rms_norm(qkv[:, :, 1], k_norm).astype(f32).transpose(0, 2, 1, 3)
    v = qkv[:, :, 2].astype(f32).transpose(0, 2, 1, 3)
    n_blocks = s // SB_BLOCK
    qb = q.reshape(b, SB_HEADS, n_blocks, SB_BLOCK, SB_HEAD_DIM).transpose(2, 0, 1, 3, 4)
    inv_sqrt_d = 1.0 / math.sqrt(SB_HEAD_DIM)
    key_pos = jnp.arange(s)

    def block(args):
        q_blk, blk = args
        z = jnp.einsum('bhqd,bhkd->bhqk', q_blk, k) * inv_sqrt_d
        t = blk * SB_BLOCK + jnp.arange(SB_BLOCK)
        mask = key_pos[None, :] < t[:, None]
        log_1m = jnp.where(mask, jax.nn.log_sigmoid(-z), 0.0)
        after = lax.cumsum(log_1m, axis=3, reverse=True) - log_1m
        a = jnp.where(mask, jnp.exp(jax.nn.log_sigmoid(z) + after), 0.0)
        return jnp.einsum('bhqk,bhkd->bhqd', a, v)

    o = lax.map(block, (qb, jnp.arange(n_blocks)))
    o = o.transpose(1, 0, 3, 2, 4).reshape(b, s, D_MODEL)
    return o.astype(h.dtype) @ w_out


def swiglu(h, w_gate, w_up, w_down):
    return (jax.nn.silu(h @ w_gate) * (h @ w_up)) @ w_down


def setup_inputs(seed: int = 0) -> dict:
    key = jax.random.key(seed)
    ks = jax.random.split(key, 24)
    f32 = jnp.float32

    def nrm(k, shape, scale):
        return jax.random.normal(k, shape, f32) * scale

    def gain(k, shape):
        return 1.0 + 0.02 * jax.random.normal(k, shape, f32)

    dt0 = jnp.exp(jax.random.uniform(ks[10], (N_SSD_LAYERS, SSD_HEADS), f32, math.log(1e-3), math.log(1e-1)))
    return {
        "x": nrm(ks[0], (BATCH, SEQ, D_MODEL), 1.0),
        "mix_norm": gain(ks[1], (DEPTH, D_MODEL)),
        "pool_in": nrm(ks[2], (N_POOL_LAYERS, D_MODEL, D_MODEL), D_MODEL ** -0.5),
        "pool_group": nrm(ks[3], (N_POOL_LAYERS, POOL_GROUPS, POOL_GROUP_DIM, POOL_GROUP_DIM), POOL_GROUP_DIM ** -0.5),
        "pool_scale": gain(ks[4], (N_POOL_LAYERS, D_MODEL)),
        "ssd_in": nrm(ks[5], (N_SSD_LAYERS, D_MODEL, SSD_IN_DIM), D_MODEL ** -0.5),
        "ssd_conv_w": nrm(ks[6], (N_SSD_LAYERS, SSD_CONV, SSD_CONV_CH), SSD_CONV ** -0.5),
        "ssd_conv_b": nrm(ks[7], (N_SSD_LAYERS, SSD_CONV_CH), 0.01),
        "ssd_dt_bias": dt0 + jnp.log(-jnp.expm1(-dt0)),
        "ssd_a_log": jnp.log(jax.random.uniform(ks[8], (N_SSD_LAYERS, SSD_HEADS), f32, 1.0, 16.0)),
        "ssd_d": gain(ks[9], (N_SSD_LAYERS, SSD_HEADS)),
        "ssd_out_norm": gain(ks[11], (N_SSD_LAYERS, SSD_D_INNER)),
        "ssd_out": nrm(ks[12], (N_SSD_LAYERS, SSD_D_INNER, D_MODEL), SSD_D_INNER ** -0.5),
        "sb_qkv": nrm(ks[13], (N_SB_LAYERS, D_MODEL, 3 * D_MODEL), D_MODEL ** -0.5),
        "sb_q_norm": gain(ks[14], (N_SB_LAYERS, SB_HEAD_DIM)),
        "sb_k_norm": gain(ks[15], (N_SB_LAYERS, SB_HEAD_DIM)),
        "sb_out": nrm(ks[16], (N_SB_LAYERS, D_MODEL, D_MODEL), D_MODEL ** -0.5),
        "ffn_norm": gain(ks[17], (DEPTH, D_MODEL)),
        "ffn_gate": nrm(ks[18], (DEPTH, D_MODEL, FFN_HIDDEN), D_MODEL ** -0.5),
        "ffn_up": nrm(ks[19], (DEPTH, D_MODEL, FFN_HIDDEN), D_MODEL ** -0.5),
        "ffn_down": nrm(ks[20], (DEPTH, FFN_HIDDEN, D_MODEL), FFN_HIDDEN ** -0.5),
    }


def reference(x, mix_norm, pool_in, pool_group, pool_scale, ssd_in, ssd_conv_w, ssd_conv_b,
              ssd_dt_bias, ssd_a_log, ssd_d, ssd_out_norm, ssd_out, sb_qkv, sb_q_norm, sb_k_norm,
              sb_out, ffn_norm, ffn_gate, ffn_up, ffn_down):
    for i in range(DEPTH):
        kind, j = i % N_MIXERS, i // N_MIXERS
        h = rms_norm(x, mix_norm[i])
        if kind == 0:
            m = pool_mixer(h, pool_in[j], pool_group[j], pool_scale[j])
        elif kind == 1:
            m = ssd_mixer(h, ssd_in[j], ssd_conv_w[j], ssd_conv_b[j], ssd_dt_bias[j], ssd_a_log[j],
                          ssd_d[j], ssd_out_norm[j], ssd_out[j])
        else:
            m = stick_breaking_mixer(h, sb_qkv[j], sb_q_norm[j], sb_k_norm[j], sb_out[j])
        x = x + m
        h = rms_norm(x, ffn_norm[i])
        x = x + swiglu(h, ffn_gate[i], ffn_up[i], ffn_down[i])
    return x
```

```python
import functools
import math

import jax
import jax.numpy as jnp
from jax import lax
from jax.experimental import pallas as pl
from jax.experimental.pallas import tpu as pltpu

F32 = jnp.float32
BF16 = jnp.bfloat16

NORM_EPS = 1e-6
POOL_WINDOWS = (2, 4, 8, 16)
POOL_HALO = 16
SSD_HEAD_DIM = 64
SSD_GROUPS = 8
SSD_HEADS_PER_GROUP = 4
SSD_STATE = 128
SSD_CONV = 4
SSD_CHUNK = 256
CONV_HALO = 8
SB_HEADS = 16
SB_HEAD_DIM = 64
SB_TILE = 256
NEG_BIG = -1e30

VMEM_LIMIT = 48 * 1024 * 1024


def _params(*sem):
    return pltpu.CompilerParams(dimension_semantics=sem, vmem_limit_bytes=VMEM_LIMIT)


def _rms_norm(x, gain):
    return x * lax.rsqrt(jnp.mean(x * x, axis=-1, keepdims=True) + NORM_EPS) * gain


def _silu(x):
    return x / (1.0 + jnp.exp(-x))


def _softplus(x):
    return jnp.maximum(x, 0.0) + jnp.log(1.0 + jnp.exp(-jnp.abs(x)))


def _dot(a, b):
    return jnp.dot(a, b, preferred_element_type=F32)


def _dot_nt(a, b):
    return lax.dot_general(a, b, (((1,), (1,)), ((), ())), preferred_element_type=F32)


def _dot_tn(a, b):
    return lax.dot_general(a, b, (((0,), (0,)), ((), ())), preferred_element_type=F32)


def _split3(a):
    hi = a.astype(BF16)
    r = a - hi.astype(F32)
    mid = r.astype(BF16)
    lo = (r - mid.astype(F32)).astype(BF16)
    return hi, mid, lo


def _norm_matmul_kernel(x_ref, g_ref, w_ref, o_ref, h_ref):
    @pl.when(pl.program_id(1) == 0)
    def _():
        h_ref[...] = _rms_norm(x_ref[...], g_ref[...]).astype(BF16)

    o_ref[...] = _dot(h_ref[...], w_ref[...]).astype(o_ref.dtype)


def _norm_matmul(x, gain, w, tm, tn, out_dtype=F32):
    t, d = x.shape
    n = w.shape[1]
    return pl.pallas_call(
        _norm_matmul_kernel,
        grid=(t // tm, n // tn),
        in_specs=[pl.BlockSpec((tm, d), lambda i, j: (i, 0)),
                  pl.BlockSpec((1, d), lambda i, j: (0, 0)),
                  pl.BlockSpec((d, tn), lambda i, j: (0, j))],
        out_specs=pl.BlockSpec((tm, tn), lambda i, j: (i, j)),
        out_shape=jax.ShapeDtypeStruct((t, n), out_dtype),
        scratch_shapes=[pltpu.VMEM((tm, d), BF16)],
        compiler_params=_params("parallel", "arbitrary"),
        name="norm_matmul",
    )(x, gain.reshape(1, d), w)


def _matmul_residual_kernel(a_ref, w_ref, r_ref, o_ref):
    o_ref[...] = r_ref[...] + _dot(a_ref[...].astype(BF16), w_ref[...])


def _matmul_residual(a, w, res, tm):
    t, k = a.shape
    n = w.shape[1]
    return pl.pallas_call(
        _matmul_residual_kernel,
        grid=(t // tm,),
        in_specs=[pl.BlockSpec((tm, k), lambda i: (i, 0)),
                  pl.BlockSpec((k, n), lambda i: (0, 0)),
                  pl.BlockSpec((tm, n), lambda i: (i, 0))],
        out_specs=pl.BlockSpec((tm, n), lambda i: (i, 0)),
        out_shape=jax.ShapeDtypeStruct((t, n), F32),
        compiler_params=_params("parallel"),
        name="matmul_residual",
    )(a, w, res)


def _ffn_kernel(x_ref, g_ref, wg_ref, wu_ref, wd_ref, o_ref, h_ref, acc_ref):
    k = pl.program_id(1)

    @pl.when(k == 0)
    def _():
        h_ref[...] = _rms_norm(x_ref[...], g_ref[...]).astype(BF16)
        acc_ref[...] = jnp.zeros_like(acc_ref)

    h = h_ref[...]
    a = _silu(_dot(h, wg_ref[...])) * _dot(h, wu_ref[...])
    acc_ref[...] += _dot(a.astype(BF16), wd_ref[...])

    @pl.when(k == pl.num_programs(1) - 1)
    def _():
        o_ref[...] = x_ref[...] + acc_ref[...]


def _ffn(x, gain, wg, wu, wd, tm, th):
    t, d = x.shape
    hid = wg.shape[1]
    return pl.pallas_call(
        _ffn_kernel,
        grid=(t // tm, hid // th),
        in_specs=[pl.BlockSpec((tm, d), lambda i, k: (i, 0)),
                  pl.BlockSpec((1, d), lambda i, k: (0, 0)),
                  pl.BlockSpec((d, th), lambda i, k: (0, k)),
                  pl.BlockSpec((d, th), lambda i, k: (0, k)),
                  pl.BlockSpec((th, d), lambda i, k: (k, 0))],
        out_specs=pl.BlockSpec((tm, d), lambda i, k: (i, 0)),
        out_shape=jax.ShapeDtypeStruct((t, d), F32),
        scratch_shapes=[pltpu.VMEM((tm, d), BF16), pltpu.VMEM((tm, d), F32)],
        compiler_params=_params("parallel", "arbitrary"),
        name="ffn",
    )(x, gain.reshape(1, d), wg, wu, wd)


def _pool_kernel(x_ref, g_ref, win_ref, wgrp_ref, sc_ref, o_ref, carry_ref, *, ts, gd):
    s = pl.program_id(1)

    @pl.when(s == 0)
    def _():
        carry_ref[...] = jnp.zeros_like(carry_ref)

    x = x_ref[0]
    h = _rms_norm(x, g_ref[...]).astype(BF16)
    u = _dot(h, win_ref[...])
    ext = jnp.concatenate([carry_ref[...], u], axis=0)
    carry_ref[...] = u[ts - POOL_HALO:, :]

    pos = s * ts + lax.broadcasted_iota(jnp.int32, (ts, 1), 0)
    for g, w in enumerate(POOL_WINDOWS):
        cols = slice(g * gd, (g + 1) * gd)
        acc = ext[:, cols]
        span = 1
        while span < w:
            acc = acc + pltpu.roll(acc, span, axis=0)
            span *= 2
        inv_count = 1.0 / jnp.minimum(pos + 1, w).astype(F32)
        p = acc[POOL_HALO:, :] * inv_count - u[:, cols]
        y = _dot(p.astype(BF16), wgrp_ref[g])
        o_ref[0, :, cols] = x[:, cols] + y * sc_ref[:, cols]


def _pool_layer(x, gain, w_in, w_group, scale, ts):
    b, s, d = x.shape
    ng, gd, _ = w_group.shape
    kern = functools.partial(_pool_kernel, ts=ts, gd=gd)
    return pl.pallas_call(
        kern,
        grid=(b, s // ts),
        in_specs=[pl.BlockSpec((1, ts, d), lambda i, j: (i, j, 0)),
                  pl.BlockSpec((1, d), lambda i, j: (0, 0)),
                  pl.BlockSpec((d, d), lambda i, j: (0, 0)),
                  pl.BlockSpec((ng, gd, gd), lambda i, j: (0, 0, 0)),
                  pl.BlockSpec((1, d), lambda i, j: (0, 0))],
        out_specs=pl.BlockSpec((1, ts, d), lambda i, j: (i, j, 0)),
        out_shape=jax.ShapeDtypeStruct((b, s, d), F32),
        scratch_shapes=[pltpu.VMEM((POOL_HALO, d), F32)],
        compiler_params=_params("parallel", "arbitrary"),
        name="pool_mixer",
    )(x, gain.reshape(1, d), w_in, w_group, scale.reshape(1, d))


def _ssd_kernel(z_ref, xp_ref, bp_ref, cp_ref, dtc_ref, dtr_ref, res_ref,
                cwx_ref, cwb_ref, cwc_ref, cbx_ref, cbb_ref, cbc_ref,
                dtb_r_ref, dtb_c_ref, alog_r_ref, alog_c_ref, dsk_ref, onorm_ref, wout_ref,
                o_ref, state_ref, carry_ref, acc_ref):
    c = pl.program_id(1)
    g = pl.program_id(2)
    L = SSD_CHUNK
    P = SSD_HEAD_DIM
    HG = SSD_HEADS_PER_GROUP
    GW = HG * P
    N = SSD_STATE

    @pl.when(c == 0)
    def _():
        state_ref[g] = jnp.zeros((GW, N), F32)
        carry_ref[g] = jnp.zeros((CONV_HALO, GW + 2 * N), F32)

    @pl.when(g == 0)
    def _():
        acc_ref[...] = jnp.zeros_like(acc_ref)

    def conv_silu(raw, prev, w_ref, b_ref):
        ext = jnp.concatenate([prev, raw], axis=0)
        out = ext * w_ref[SSD_CONV - 1:SSD_CONV, :]
        for k in range(SSD_CONV - 1):
            out = out + pltpu.roll(ext, SSD_CONV - 1 - k, axis=0) * w_ref[k:k + 1, :]
        return _silu(out[CONV_HALO:, :] + b_ref[...])

    x_raw, b_raw, c_raw = xp_ref[0], bp_ref[0], cp_ref[0]
    prev = carry_ref[g]
    xs = conv_silu(x_raw, prev[:, :GW], cwx_ref, cbx_ref)
    bm = conv_silu(b_raw, prev[:, GW:GW + N], cwb_ref, cbb_ref)
    cm = conv_silu(c_raw, prev[:, GW + N:], cwc_ref, cbc_ref)
    carry_ref[g] = jnp.concatenate(
        [x_raw[L - CONV_HALO:], b_raw[L - CONV_HALO:], c_raw[L - CONV_HALO:]], axis=1)

    dt_c = _softplus(dtc_ref[0] + dtb_r_ref[...])
    dt_r = _softplus(dtr_ref[0] + dtb_c_ref[...])
    da_c = dt_c * (-jnp.exp(alog_r_ref[...]))
    da_r = dt_r * (-jnp.exp(alog_c_ref[...]))
    row = lax.broadcasted_iota(jnp.int32, (L, L), 0)
    col = lax.broadcasted_iota(jnp.int32, (L, L), 1)
    causal = row >= col
    tri = jnp.where(causal, 1.0, 0.0).astype(BF16)
    tri_t = jnp.where(row <= col, 1.0, 0.0).astype(BF16)
    acum_c = sum(_dot(tri, piece) for piece in _split3(da_c))
    acum_r = sum(_dot(piece, tri_t) for piece in _split3(da_r))
    nh = acum_c.shape[1]
    lane_h = lax.broadcasted_iota(jnp.int32, (1, nh), 1)
    sub_h = lax.broadcasted_iota(jnp.int32, (nh, 1), 0)

    cb = _dot_nt(cm.astype(BF16), bm.astype(BF16))
    cstate = _dot_nt(cm.astype(BF16), state_ref[g].astype(BF16))
    bm_bf = bm.astype(BF16)

    ys, xws, decs = [], [], []
    for hh in range(HG):
        head = g * HG + hh
        pick_c = lane_h == head
        a_col = jnp.sum(jnp.where(pick_c, acum_c, 0.0), axis=1, keepdims=True)
        dt_col = jnp.sum(jnp.where(pick_c, dt_c, 0.0), axis=1, keepdims=True)
        a_row = jnp.sum(jnp.where(sub_h == head, acum_r, 0.0), axis=0, keepdims=True)
        a_last = a_col[L - 1:L, :]
        d_h = jnp.sum(jnp.where(pick_c, dsk_ref[...], 0.0), axis=1, keepdims=True)

        x_h = xs[:, hh * P:(hh + 1) * P]
        xdt = x_h * dt_col
        decay = jnp.exp(jnp.where(causal, a_col - a_row, NEG_BIG))
        y_h = _dot((cb * decay).astype(BF16), xdt.astype(BF16))
        y_h = y_h + cstate[:, hh * P:(hh + 1) * P] * jnp.exp(a_col) + d_h * x_h
        ys.append(y_h)
        xws.append(xdt * jnp.exp(a_last - a_col))
        decs.append(jnp.broadcast_to(jnp.exp(a_last), (P, 1)))

    xw = jnp.concatenate(xws, axis=1)
    dec = jnp.concatenate(decs, axis=0)
    state_ref[g] = state_ref[g] * dec + _dot_tn(xw.astype(BF16), bm_bf)

    y = jnp.concatenate(ys, axis=1)
    gated = y * _silu(z_ref[0])
    gated = gated * lax.rsqrt(jnp.mean(gated * gated, axis=-1, keepdims=True) + NORM_EPS)
    gated = gated * onorm_ref[...]
    acc_ref[...] += _dot(gated.astype(BF16), wout_ref[...])

    @pl.when(g == pl.num_programs(2) - 1)
    def _():
        o_ref[0] = res_ref[0] + acc_ref[...]


def _ssd_layer(x, gain, w_in, conv_w, conv_b, dt_bias, a_log, d_skip, out_norm, w_out):
    b, s, d = x.shape
    L, G, N = SSD_CHUNK, SSD_GROUPS, SSD_STATE
    GW = SSD_HEADS_PER_GROUP * SSD_HEAD_DIM
    d_inner = G * GW
    gn = G * N
    nh = G * SSD_HEADS_PER_GROUP
    x2 = x.reshape(b * s, d)
    w_z = w_in[:, :d_inner]
    w_xbc = w_in[:, d_inner:2 * d_inner + 2 * gn]
    w_dt = w_in[:, 2 * d_inner + 2 * gn:]
    z = _norm_matmul(x2, gain, w_z.astype(BF16), 512, 512).reshape(b, s, d_inner)
    xbc = _norm_matmul(x2, gain, w_xbc.astype(BF16), 512, 512).reshape(b, s, d_inner + 2 * gn)
    w_dt_pad = jnp.pad(w_dt, ((0, 0), (0, 128 - nh))).astype(BF16)
    dt = _norm_matmul(x2, gain, w_dt_pad, 512, 128)[:, :nh].reshape(b, s, nh)
    dt_t = jnp.swapaxes(dt, 1, 2)

    xoff, boff, coff = 0, d_inner // N, (d_inner + gn) // N
    row = lambda v: v.reshape(1, -1)
    in_specs = [
        pl.BlockSpec((1, L, GW), lambda i, c, g: (i, c, g)),
        pl.BlockSpec((1, L, GW), lambda i, c, g: (i, c, g)),
        pl.BlockSpec((1, L, N), lambda i, c, g: (i, c, boff + g)),
        pl.BlockSpec((1, L, N), lambda i, c, g: (i, c, coff + g)),
        pl.BlockSpec((1, L, nh), lambda i, c, g: (i, c, 0)),
        pl.BlockSpec((1, nh, L), lambda i, c, g: (i, 0, c)),
        pl.BlockSpec((1, L, d), lambda i, c, g: (i, c, 0)),
        pl.BlockSpec((SSD_CONV, GW), lambda i, c, g: (0, g)),
        pl.BlockSpec((SSD_CONV, N), lambda i, c, g: (0, boff + g)),
        pl.BlockSpec((SSD_CONV, N), lambda i, c, g: (0, coff + g)),
        pl.BlockSpec((1, GW), lambda i, c, g: (0, g)),
        pl.BlockSpec((1, N), lambda i, c, g: (0, boff + g)),
        pl.BlockSpec((1, N), lambda i, c, g: (0, coff + g)),
        pl.BlockSpec((1, nh), lambda i, c, g: (0, 0)),
        pl.BlockSpec((nh, 1), lambda i, c, g: (0, 0)),
        pl.BlockSpec((1, nh), lambda i, c, g: (0, 0)),
        pl.BlockSpec((nh, 1), lambda i, c, g: (0, 0)),
        pl.BlockSpec((1, nh), lambda i, c, g: (0, 0)),
        pl.BlockSpec((1, GW), lambda i, c, g: (0, g)),
        pl.BlockSpec((GW, d), lambda i, c, g: (g, 0)),
    ]
    return pl.pallas_call(
        _ssd_kernel,
        grid=(b, s // L, G),
        in_specs=in_specs,
        out_specs=pl.BlockSpec((1, L, d), lambda i, c, g: (i, c, 0)),
        out_shape=jax.ShapeDtypeStruct((b, s, d), F32),
        scratch_shapes=[pltpu.VMEM((G, GW, N), F32),
                        pltpu.VMEM((G, CONV_HALO, GW + 2 * N), F32),
                        pltpu.VMEM((L, d), F32)],
        compiler_params=_params("parallel", "arbitrary", "arbitrary"),
        name="ssd_mixer",
    )(z, xbc, xbc, xbc, dt, dt_t, x,
      conv_w, conv_w, conv_w, row(conv_b), row(conv_b), row(conv_b),
      row(dt_bias), dt_bias.reshape(-1, 1), row(a_log), a_log.reshape(-1, 1),
      row(d_skip), row(out_norm), w_out.astype(BF16))


def _sb_kernel(q_ref, k_ref, v_ref, qg_ref, kg_ref, o_ref, kn_ref, vb_ref):
    qi = pl.program_id(2)
    T = SB_TILE

    def head_norm(t, gain):
        return t * lax.rsqrt(jnp.mean(t * t, axis=-1, keepdims=True) + NORM_EPS) * gain

    @pl.when(qi == 0)
    def _():
        kn_ref[...] = head_norm(k_ref[0, 0], kg_ref[...]).astype(BF16)
        vb_ref[...] = v_ref[0, 0].astype(BF16)

    q = (head_norm(q_ref[0, 0], qg_ref[...]) * (1.0 / math.sqrt(SB_HEAD_DIM))).astype(BF16)
    row = lax.broadcasted_iota(jnp.int32, (T, T), 0)
    col = lax.broadcasted_iota(jnp.int32, (T, T), 1)
    later = jnp.where(row > col, 1.0, 0.0).astype(BF16)

    def block(j, carry, diagonal):
        o, run = carry
        start = pl.multiple_of(j * T, T)
        kb = kn_ref[pl.ds(start, T), :]
        vb = vb_ref[pl.ds(start, T), :]
        z = _dot_nt(q, kb)
        sp = _softplus(z)
        log_1m = -sp
        if diagonal:
            log_1m = jnp.where(row > col, log_1m, 0.0)
        hi = log_1m.astype(BF16)
        lo = (log_1m - hi.astype(F32)).astype(BF16)
        after = run + _dot(hi, later) + _dot(lo, later)
        a = jnp.exp(z - sp + after)
        if diagonal:
            a = jnp.where(row > col, a, 0.0)
        o = o + _dot(a.astype(BF16), vb)
        run = run + jnp.sum(log_1m, axis=1, keepdims=True)
        return o, run

    carry = (jnp.zeros((T, SB_HEAD_DIM), F32), jnp.zeros((T, 1), F32))
    carry = block(qi, carry, True)
    o, _ = lax.fori_loop(0, qi, lambda n, cr: block(qi - 1 - n, cr, False), carry)
    o_ref[0, 0] = o


def _sb_attention(q, k, v, q_gain, k_gain):
    b, h, s, dh = q.shape
    T = SB_TILE
    return pl.pallas_call(
        _sb_kernel,
        grid=(b, h, s // T),
        in_specs=[pl.BlockSpec((1, 1, T, dh), lambda i, j, t: (i, j, t, 0)),
                  pl.BlockSpec((1, 1, s, dh), lambda i, j, t: (i, j, 0, 0)),
                  pl.BlockSpec((1, 1, s, dh), lambda i, j, t: (i, j, 0, 0)),
                  pl.BlockSpec((1, dh), lambda i, j, t: (0, 0)),
                  pl.BlockSpec((1, dh), lambda i, j, t: (0, 0))],
        out_specs=pl.BlockSpec((1, 1, T, dh), lambda i, j, t: (i, j, t, 0)),
        out_shape=jax.ShapeDtypeStruct((b, h, s, dh), F32),
        scratch_shapes=[pltpu.VMEM((s, dh), BF16), pltpu.VMEM((s, dh), BF16)],
        compiler_params=_params("parallel", "parallel", "arbitrary"),
        name="sb_attention",
    )(q, k, v, q_gain.reshape(1, dh), k_gain.reshape(1, dh))


def _sb_layer(x, gain, w_qkv, q_gain, k_gain, w_out):
    b, s, d = x.shape
    x2 = x.reshape(b * s, d)
    qkv = _norm_matmul(x2, gain, w_qkv.astype(BF16), 512, 512)
    qkv = qkv.reshape(b, s, 3, SB_HEADS, SB_HEAD_DIM).transpose(2, 0, 3, 1, 4)
    o = _sb_attention(qkv[0], qkv[1], qkv[2], q_gain, k_gain)
    o = o.transpose(0, 2, 1, 3).reshape(b * s, d)
    return _matmul_residual(o, w_out.astype(BF16), x2, 512).reshape(b, s, d)


def kernel(x, mix_norm, pool_in, pool_group, pool_scale, ssd_in, ssd_conv_w, ssd_conv_b, ssd_dt_bias,
           ssd_a_log, ssd_d, ssd_out_norm, ssd_out, sb_qkv, sb_q_norm, sb_k_norm, sb_out, ffn_norm,
           ffn_gate, ffn_up, ffn_down):
    b, s, d = x.shape
    depth = mix_norm.shape[0]
    for i in range(depth):
        kind, j = i % 3, i // 3
        if kind == 0:
            x = _pool_layer(x, mix_norm[i], pool_in[j].astype(BF16), pool_group[j].astype(BF16),
                            pool_scale[j], 512)
        elif kind == 1:
            x = _ssd_layer(x, mix_norm[i], ssd_in[j], ssd_conv_w[j], ssd_conv_b[j], ssd_dt_bias[j],
                           ssd_a_log[j], ssd_d[j], ssd_out_norm[j], ssd_out[j])
        else:
            x = _sb_layer(x, mix_norm[i], sb_qkv[j], sb_q_norm[j], sb_k_norm[j], sb_out[j])
        x = _ffn(x.reshape(b * s, d), ffn_norm[i], ffn_gate[i].astype(BF16), ffn_up[i].astype(BF16),
                 ffn_down[i].astype(BF16), 512, 256).reshape(b, s, d)
    return x
```

```python
import functools
import math

import jax
import jax.numpy as jnp
from jax import lax
from jax.experimental import pallas as pl
from jax.experimental.pallas import tpu as pltpu

F32 = jnp.float32
BF16 = jnp.bfloat16

NORM_EPS = 1e-6
POOL_WINDOWS = (2, 4, 8, 16)
POOL_HALO = 16
SSD_HEAD_DIM = 64
SSD_GROUPS = 8
SSD_HEADS_PER_GROUP = 4
SSD_STATE = 128
SSD_CONV = 4
SSD_CHUNK = 256
CONV_HALO = 8
SB_HEADS = 16
SB_HEAD_DIM = 64
SB_TILE = 256
SB_HEAD_PAIRS = 2
SB_EXIT = -152.0
LANES = 128
NEG_BIG = -1e30

VMEM_LIMIT = 56 * 1024 * 1024


def _params(*sem):
    return pltpu.CompilerParams(dimension_semantics=sem, vmem_limit_bytes=VMEM_LIMIT)


def _rms_norm(x, gain):
    return x * lax.rsqrt(jnp.mean(x * x, axis=-1, keepdims=True) + NORM_EPS) * gain


def _silu(x):
    return x / (1.0 + jnp.exp(-x))


def _softplus(x):
    return jnp.maximum(x, 0.0) + jnp.log(1.0 + jnp.exp(-jnp.abs(x)))


def _dot(a, b):
    return jnp.dot(a, b, preferred_element_type=F32)


def _dot_nt(a, b):
    return lax.dot_general(a, b, (((1,), (1,)), ((), ())), preferred_element_type=F32)


def _dot_tn(a, b):
    return lax.dot_general(a, b, (((0,), (0,)), ((), ())), preferred_element_type=F32)


def _split3(a):
    hi = a.astype(BF16)
    r = a - hi.astype(F32)
    mid = r.astype(BF16)
    lo = (r - mid.astype(F32)).astype(BF16)
    return hi, mid, lo


def _norm_matmul_kernel(x_ref, g_ref, w_ref, o_ref, h_ref):
    @pl.when(pl.program_id(1) == 0)
    def _():
        h_ref[...] = _rms_norm(x_ref[...], g_ref[...]).astype(BF16)

    o_ref[...] = _dot(h_ref[...], w_ref[...]).astype(o_ref.dtype)


def _norm_matmul(x, gain, w, tm, tn, out_dtype=F32):
    t, d = x.shape
    n = w.shape[1]
    return pl.pallas_call(
        _norm_matmul_kernel,
        grid=(t // tm, n // tn),
        in_specs=[pl.BlockSpec((tm, d), lambda i, j: (i, 0)),
                  pl.BlockSpec((1, d), lambda i, j: (0, 0)),
                  pl.BlockSpec((d, tn), lambda i, j: (0, j))],
        out_specs=pl.BlockSpec((tm, tn), lambda i, j: (i, j)),
        out_shape=jax.ShapeDtypeStruct((t, n), out_dtype),
        scratch_shapes=[pltpu.VMEM((tm, d), BF16)],
        compiler_params=_params("parallel", "arbitrary"),
        name="norm_matmul",
    )(x, gain.reshape(1, d), w)


def _matmul_residual_kernel(a_ref, w_ref, r_ref, o_ref):
    o_ref[...] = r_ref[...] + _dot(a_ref[...].astype(BF16), w_ref[...])


def _matmul_residual(a, w, res, tm):
    t, k = a.shape
    n = w.shape[1]
    return pl.pallas_call(
        _matmul_residual_kernel,
        grid=(t // tm,),
        in_specs=[pl.BlockSpec((tm, k), lambda i: (i, 0)),
                  pl.BlockSpec((k, n), lambda i: (0, 0)),
                  pl.BlockSpec((tm, n), lambda i: (i, 0))],
        out_specs=pl.BlockSpec((tm, n), lambda i: (i, 0)),
        out_shape=jax.ShapeDtypeStruct((t, n), F32),
        compiler_params=_params("parallel"),
        name="matmul_residual",
    )(a, w, res)


def _ffn_kernel(x_ref, g_ref, wg_ref, wu_ref, wd_ref, o_ref, h_ref, acc_ref):
    k = pl.program_id(1)

    @pl.when(k == 0)
    def _():
        h_ref[...] = _rms_norm(x_ref[...], g_ref[...]).astype(BF16)
        acc_ref[...] = jnp.zeros_like(acc_ref)

    h = h_ref[...]
    a = _silu(_dot(h, wg_ref[...])) * _dot(h, wu_ref[...])
    acc_ref[...] += _dot(a.astype(BF16), wd_ref[...])

    @pl.when(k == pl.num_programs(1) - 1)
    def _():
        o_ref[...] = x_ref[...] + acc_ref[...]


def _ffn(x, gain, wg, wu, wd, tm, th):
    t, d = x.shape
    hid = wg.shape[1]
    return pl.pallas_call(
        _ffn_kernel,
        grid=(t // tm, hid // th),
        in_specs=[pl.BlockSpec((tm, d), lambda i, k: (i, 0)),
                  pl.BlockSpec((1, d), lambda i, k: (0, 0)),
                  pl.BlockSpec((d, th), lambda i, k: (0, k)),
                  pl.BlockSpec((d, th), lambda i, k: (0, k)),
                  pl.BlockSpec((th, d), lambda i, k: (k, 0))],
        out_specs=pl.BlockSpec((tm, d), lambda i, k: (i, 0)),
        out_shape=jax.ShapeDtypeStruct((t, d), F32),
        scratch_shapes=[pltpu.VMEM((tm, d), BF16), pltpu.VMEM((tm, d), F32)],
        compiler_params=_params("parallel", "arbitrary"),
        name="ffn",
    )(x, gain.reshape(1, d), wg, wu, wd)


def _ffn_resident_kernel(x_ref, g_ref, wg_ref, wu_ref, wd_ref, o_ref, *, th):
    x = x_ref[...]
    h = _rms_norm(x, g_ref[...]).astype(BF16)
    acc = x
    for k in range(wg_ref.shape[1] // th):
        cols = slice(k * th, (k + 1) * th)
        a = _silu(_dot(h, wg_ref[:, cols])) * _dot(h, wu_ref[:, cols])
        acc = acc + _dot(a.astype(BF16), wd_ref[cols, :])
    o_ref[...] = acc


def _ffn_resident(x, gain, wg, wu, wd, tm, th):
    t, d = x.shape
    hid = wg.shape[1]
    once = pl.Buffered(1)
    return pl.pallas_call(
        functools.partial(_ffn_resident_kernel, th=th),
        grid=(t // tm,),
        in_specs=[pl.BlockSpec((tm, d), lambda i: (i, 0)),
                  pl.BlockSpec((1, d), lambda i: (0, 0)),
                  pl.BlockSpec((d, hid), lambda i: (0, 0), pipeline_mode=once),
                  pl.BlockSpec((d, hid), lambda i: (0, 0), pipeline_mode=once),
                  pl.BlockSpec((hid, d), lambda i: (0, 0), pipeline_mode=once)],
        out_specs=pl.BlockSpec((tm, d), lambda i: (i, 0)),
        out_shape=jax.ShapeDtypeStruct((t, d), F32),
        compiler_params=_params("parallel"),
        name="ffn_resident",
    )(x, gain.reshape(1, d), wg, wu, wd)


def _pool_kernel(x_ref, g_ref, win_ref, wgrp_ref, sc_ref, o_ref, carry_ref, *, ts, gd):
    s = pl.program_id(1)

    @pl.when(s == 0)
    def _():
        carry_ref[...] = jnp.zeros_like(carry_ref)

    x = x_ref[0]
    h = _rms_norm(x, g_ref[...]).astype(BF16)
    u = _dot(h, win_ref[...])
    ext = jnp.concatenate([carry_ref[...], u], axis=0)
    carry_ref[...] = u[ts - POOL_HALO:, :]

    pos = s * ts + lax.broadcasted_iota(jnp.int32, (ts, 1), 0)
    for g, w in enumerate(POOL_WINDOWS):
        cols = slice(g * gd, (g + 1) * gd)
        acc = ext[:, cols]
        span = 1
        while span < w:
            acc = acc + pltpu.roll(acc, span, axis=0)
            span *= 2
        inv_count = 1.0 / jnp.minimum(pos + 1, w).astype(F32)
        p = acc[POOL_HALO:, :] * inv_count - u[:, cols]
        y = _dot(p.astype(BF16), wgrp_ref[g])
        o_ref[0, :, cols] = x[:, cols] + y * sc_ref[:, cols]


def _pool_layer(x, gain, w_in, w_group, scale, ts):
    b, s, d = x.shape
    ng, gd, _ = w_group.shape
    kern = functools.partial(_pool_kernel, ts=ts, gd=gd)
    return pl.pallas_call(
        kern,
        grid=(b, s // ts),
        in_specs=[pl.BlockSpec((1, ts, d), lambda i, j: (i, j, 0)),
                  pl.BlockSpec((1, d), lambda i, j: (0, 0)),
                  pl.BlockSpec((d, d), lambda i, j: (0, 0)),
                  pl.BlockSpec((ng, gd, gd), lambda i, j: (0, 0, 0)),
                  pl.BlockSpec((1, d), lambda i, j: (0, 0))],
        out_specs=pl.BlockSpec((1, ts, d), lambda i, j: (i, j, 0)),
        out_shape=jax.ShapeDtypeStruct((b, s, d), F32),
        scratch_shapes=[pltpu.VMEM((POOL_HALO, d), F32)],
        compiler_params=_params("parallel", "arbitrary"),
        name="pool_mixer",
    )(x, gain.reshape(1, d), w_in, w_group, scale.reshape(1, d))


def _ssd_kernel(z_ref, xp_ref, bp_ref, cp_ref, dtc_ref, dtr_ref, res_ref,
                cwx_ref, cwb_ref, cwc_ref, cbx_ref, cbb_ref, cbc_ref,
                dtb_r_ref, dtb_c_ref, alog_r_ref, alog_c_ref, dsk_ref, onorm_ref, wout_ref,
                o_ref, state_ref, carry_ref, acc_ref):
    c = pl.program_id(1)
    g = pl.program_id(2)
    L = SSD_CHUNK
    P = SSD_HEAD_DIM
    HG = SSD_HEADS_PER_GROUP
    GW = HG * P
    N = SSD_STATE

    @pl.when(c == 0)
    def _():
        state_ref[g] = jnp.zeros((GW, N), F32)
        carry_ref[g] = jnp.zeros((CONV_HALO, GW + 2 * N), F32)

    @pl.when(g == 0)
    def _():
        acc_ref[...] = jnp.zeros_like(acc_ref)

    def conv_silu(raw, prev, w_ref, b_ref):
        ext = jnp.concatenate([prev, raw], axis=0)
        out = ext * w_ref[SSD_CONV - 1:SSD_CONV, :]
        for k in range(SSD_CONV - 1):
            out = out + pltpu.roll(ext, SSD_CONV - 1 - k, axis=0) * w_ref[k:k + 1, :]
        return _silu(out[CONV_HALO:, :] + b_ref[...])

    x_raw, b_raw, c_raw = xp_ref[0], bp_ref[0], cp_ref[0]
    prev = carry_ref[g]
    xs = conv_silu(x_raw, prev[:, :GW], cwx_ref, cbx_ref)
    bm = conv_silu(b_raw, prev[:, GW:GW + N], cwb_ref, cbb_ref)
    cm = conv_silu(c_raw, prev[:, GW + N:], cwc_ref, cbc_ref)
    carry_ref[g] = jnp.concatenate(
        [x_raw[L - CONV_HALO:], b_raw[L - CONV_HALO:], c_raw[L - CONV_HALO:]], axis=1)

    dt_c = _softplus(dtc_ref[0] + dtb_r_ref[...])
    dt_r = _softplus(dtr_ref[0] + dtb_c_ref[...])
    da_c = dt_c * (-jnp.exp(alog_r_ref[...]))
    da_r = dt_r * (-jnp.exp(alog_c_ref[...]))
    row = lax.broadcasted_iota(jnp.int32, (L, L), 0)
    col = lax.broadcasted_iota(jnp.int32, (L, L), 1)
    causal = row >= col
    tri = jnp.where(causal, 1.0, 0.0).astype(BF16)
    tri_t = jnp.where(row <= col, 1.0, 0.0).astype(BF16)
    acum_c = sum(_dot(tri, piece) for piece in _split3(da_c))
    acum_r = sum(_dot(piece, tri_t) for piece in _split3(da_r))
    nh = acum_c.shape[1]
    lane_h = lax.broadcasted_iota(jnp.int32, (1, nh), 1)
    sub_h = lax.broadcasted_iota(jnp.int32, (nh, 1), 0)

    cb = _dot_nt(cm.astype(BF16), bm.astype(BF16))
    cstate = _dot_nt(cm.astype(BF16), state_ref[g].astype(BF16))
    bm_bf = bm.astype(BF16)

    ys, xws, decs = [], [], []
    for hh in range(HG):
        head = g * HG + hh
        pick_c = lane_h == head
        a_col = jnp.sum(jnp.where(pick_c, acum_c, 0.0), axis=1, keepdims=True)
        dt_col = jnp.sum(jnp.where(pick_c, dt_c, 0.0), axis=1, keepdims=True)
        a_row = jnp.sum(jnp.where(sub_h == head, acum_r, 0.0), axis=0, keepdims=True)
        a_last = a_col[L - 1:L, :]
        d_h = jnp.sum(jnp.where(pick_c, dsk_ref[...], 0.0), axis=1, keepdims=True)

        x_h = xs[:, hh * P:(hh + 1) * P]
        xdt = x_h * dt_col
        decay = jnp.exp(jnp.where(causal, a_col - a_row, NEG_BIG))
        y_h = _dot((cb * decay).astype(BF16), xdt.astype(BF16))
        y_h = y_h + cstate[:, hh * P:(hh + 1) * P] * jnp.exp(a_col) + d_h * x_h
        ys.append(y_h)
        xws.append(xdt * jnp.exp(a_last - a_col))
        decs.append(jnp.broadcast_to(jnp.exp(a_last), (P, 1)))

    xw = jnp.concatenate(xws, axis=1)
    dec = jnp.concatenate(decs, axis=0)
    state_ref[g] = state_ref[g] * dec + _dot_tn(xw.astype(BF16), bm_bf)

    y = jnp.concatenate(ys, axis=1)
    gated = y * _silu(z_ref[0])
    gated = gated * lax.rsqrt(jnp.mean(gated * gated, axis=-1, keepdims=True) + NORM_EPS)
    gated = gated * onorm_ref[...]
    acc_ref[...] += _dot(gated.astype(BF16), wout_ref[...])

    @pl.when(g == pl.num_programs(2) - 1)
    def _():
        o_ref[0] = res_ref[0] + acc_ref[...]


def _ssd_layer(x, gain, w_in, conv_w, conv_b, dt_bias, a_log, d_skip, out_norm, w_out):
    b, s, d = x.shape
    L, G, N = SSD_CHUNK, SSD_GROUPS, SSD_STATE
    GW = SSD_HEADS_PER_GROUP * SSD_HEAD_DIM
    d_inner = G * GW
    gn = G * N
    nh = G * SSD_HEADS_PER_GROUP
    x2 = x.reshape(b * s, d)
    w_z = w_in[:, :d_inner]
    w_xbc = w_in[:, d_inner:2 * d_inner + 2 * gn]
    w_dt = w_in[:, 2 * d_inner + 2 * gn:]
    z = _norm_matmul(x2, gain, w_z.astype(BF16), 512, 512).reshape(b, s, d_inner)
    xbc = _norm_matmul(x2, gain, w_xbc.astype(BF16), 512, 512).reshape(b, s, d_inner + 2 * gn)
    w_dt_pad = jnp.pad(w_dt, ((0, 0), (0, 128 - nh))).astype(BF16)
    dt = _norm_matmul(x2, gain, w_dt_pad, 512, 128)[:, :nh].reshape(b, s, nh)
    dt_t = jnp.swapaxes(dt, 1, 2)

    xoff, boff, coff = 0, d_inner // N, (d_inner + gn) // N
    row = lambda v: v.reshape(1, -1)
    in_specs = [
        pl.BlockSpec((1, L, GW), lambda i, c, g: (i, c, g)),
        pl.BlockSpec((1, L, GW), lambda i, c, g: (i, c, g)),
        pl.BlockSpec((1, L, N), lambda i, c, g: (i, c, boff + g)),
        pl.BlockSpec((1, L, N), lambda i, c, g: (i, c, coff + g)),
        pl.BlockSpec((1, L, nh), lambda i, c, g: (i, c, 0)),
        pl.BlockSpec((1, nh, L), lambda i, c, g: (i, 0, c)),
        pl.BlockSpec((1, L, d), lambda i, c, g: (i, c, 0)),
        pl.BlockSpec((SSD_CONV, GW), lambda i, c, g: (0, g)),
        pl.BlockSpec((SSD_CONV, N), lambda i, c, g: (0, boff + g)),
        pl.BlockSpec((SSD_CONV, N), lambda i, c, g: (0, coff + g)),
        pl.BlockSpec((1, GW), lambda i, c, g: (0, g)),
        pl.BlockSpec((1, N), lambda i, c, g: (0, boff + g)),
        pl.BlockSpec((1, N), lambda i, c, g: (0, coff + g)),
        pl.BlockSpec((1, nh), lambda i, c, g: (0, 0)),
        pl.BlockSpec((nh, 1), lambda i, c, g: (0, 0)),
        pl.BlockSpec((1, nh), lambda i, c, g: (0, 0)),
        pl.BlockSpec((nh, 1), lambda i, c, g: (0, 0)),
        pl.BlockSpec((1, nh), lambda i, c, g: (0, 0)),
        pl.BlockSpec((1, GW), lambda i, c, g: (0, g)),
        pl.BlockSpec((GW, d), lambda i, c, g: (g, 0)),
    ]
    return pl.pallas_call(
        _ssd_kernel,
        grid=(b, s // L, G),
        in_specs=in_specs,
        out_specs=pl.BlockSpec((1, L, d), lambda i, c, g: (i, c, 0)),
        out_shape=jax.ShapeDtypeStruct((b, s, d), F32),
        scratch_shapes=[pltpu.VMEM((G, GW, N), F32),
                        pltpu.VMEM((G, CONV_HALO, GW + 2 * N), F32),
                        pltpu.VMEM((L, d), F32)],
        compiler_params=_params("parallel", "arbitrary", "arbitrary"),
        name="ssd_mixer",
    )(z, xbc, xbc, xbc, dt, dt_t, x,
      conv_w, conv_w, conv_w, row(conv_b), row(conv_b), row(conv_b),
      row(dt_bias), dt_bias.reshape(-1, 1), row(a_log), a_log.reshape(-1, 1),
      row(d_skip), row(out_norm), w_out.astype(BF16))


def _first_head_lanes():
    return lax.broadcasted_iota(jnp.int32, (1, LANES), 1) < SB_HEAD_DIM


def _qkv_kernel(x_ref, g_ref, w_ref, hg_ref, o_ref, h_ref, *, n_norm_tiles):
    j = pl.program_id(1)

    @pl.when(j == 0)
    def _():
        h_ref[...] = _rms_norm(x_ref[...], g_ref[...]).astype(BF16)

    y = _dot(h_ref[...], w_ref[...])

    @pl.when(j < n_norm_tiles)
    def _():
        first = _first_head_lanes()
        for c in range(y.shape[1] // LANES):
            cols = slice(c * LANES, (c + 1) * LANES)
            blk = y[:, cols]
            sq = blk * blk
            ms0 = jnp.sum(jnp.where(first, sq, 0.0), axis=-1, keepdims=True) * (1.0 / SB_HEAD_DIM)
            ms1 = jnp.sum(jnp.where(first, 0.0, sq), axis=-1, keepdims=True) * (1.0 / SB_HEAD_DIM)
            inv = jnp.where(first, lax.rsqrt(ms0 + NORM_EPS), lax.rsqrt(ms1 + NORM_EPS))
            o_ref[:, cols] = (blk * inv * hg_ref[:, cols]).astype(o_ref.dtype)

    @pl.when(j >= n_norm_tiles)
    def _():
        o_ref[...] = y.astype(o_ref.dtype)


def _qkv_proj(x, gain, w, head_gain, n_norm_cols, tm, tn):
    t, d = x.shape
    n = w.shape[1]
    kern = functools.partial(_qkv_kernel, n_norm_tiles=n_norm_cols // tn)
    return pl.pallas_call(
        kern,
        grid=(t // tm, n // tn),
        in_specs=[pl.BlockSpec((tm, d), lambda i, j: (i, 0)),
                  pl.BlockSpec((1, d), lambda i, j: (0, 0)),
                  pl.BlockSpec((d, tn), lambda i, j: (0, j)),
                  pl.BlockSpec((1, tn), lambda i, j: (0, j))],
        out_specs=pl.BlockSpec((tm, tn), lambda i, j: (i, j)),
        out_shape=jax.ShapeDtypeStruct((t, n), BF16),
        scratch_shapes=[pltpu.VMEM((tm, d), BF16)],
        compiler_params=_params("parallel", "arbitrary"),
        name="sb_qkv_proj",
    )(x, gain.reshape(1, d), w, head_gain)


def _sb_kernel(q_ref, k_ref, v_ref, o_ref, *, hp):
    qi = pl.program_id(2)
    T = SB_TILE
    first = _first_head_lanes()
    row = lax.broadcasted_iota(jnp.int32, (T, T), 0)
    col = lax.broadcasted_iota(jnp.int32, (T, T), 1)
    strict = row > col
    later = jnp.where(strict, 1.0, 0.0).astype(BF16)

    qs = []
    for c in range(hp):
        qc = q_ref[0, :, c * LANES:(c + 1) * LANES]
        qs.append(jnp.where(first, qc, jnp.zeros_like(qc)))
        qs.append(jnp.where(first, jnp.zeros_like(qc), qc))

    def tile(qh, kb, vb, run, diagonal):
        z = _dot_nt(qh, kb)
        sp = jnp.maximum(z, 0.0) + jnp.log2(1.0 + jnp.exp2(-jnp.abs(z)))
        log_1m = -sp
        if diagonal:
            log_1m = jnp.where(strict, log_1m, 0.0)
        after = run + _dot(log_1m.astype(BF16), later)
        a = jnp.exp2(z - sp + after)
        if diagonal:
            a = jnp.where(strict, a, 0.0)
        pv = _dot(a.astype(BF16), vb)
        return pv, run + jnp.sum(log_1m, axis=1, keepdims=True)

    def step(j, os, runs, diagonal):
        start = pl.multiple_of(j * T, T)
        new_os, new_runs = [], []
        for c in range(hp):
            kb = k_ref[0, pl.ds(start, T), c * LANES:(c + 1) * LANES]
            vb = v_ref[0, pl.ds(start, T), c * LANES:(c + 1) * LANES]
            pa, ra = tile(qs[2 * c], kb, vb, runs[2 * c], diagonal)
            pb, rb = tile(qs[2 * c + 1], kb, vb, runs[2 * c + 1], diagonal)
            new_os.append(os[c] + jnp.where(first, pa, pb))
            new_runs += [ra, rb]
        return new_os, new_runs

    def unfinished(runs):
        worst = functools.reduce(jnp.maximum, runs)
        return (jnp.max(worst) >= SB_EXIT).astype(jnp.int32)

    os = [jnp.zeros((T, LANES), F32) for _ in range(hp)]
    runs = [jnp.zeros((T, 1), F32) for _ in range(2 * hp)]
    os, runs = step(qi, os, runs, True)

    def cond(carry):
        j, go = carry[0], carry[1]
        return jnp.logical_and(j >= 0, go > 0)

    def body(carry):
        j = carry[0]
        os, runs = step(j, list(carry[2:2 + hp]), list(carry[2 + hp:]), False)
        return (j - 1, unfinished(runs), *os, *runs)

    final = lax.while_loop(cond, body, (qi - 1, unfinished(runs), *os, *runs))
    for c in range(hp):
        o_ref[0, :, c * LANES:(c + 1) * LANES] = final[2 + c]


def _sb_attention(qkv, d, hp):
    b, s, _ = qkv.shape
    T = SB_TILE
    W = LANES * hp
    nblk = d // W
    kern = functools.partial(_sb_kernel, hp=hp)
    return pl.pallas_call(
        kern,
        grid=(b, nblk, s // T),
        in_specs=[pl.BlockSpec((1, T, W), lambda i, j, t: (i, t, j)),
                  pl.BlockSpec((1, s, W), lambda i, j, t: (i, 0, nblk + j)),
                  pl.BlockSpec((1, s, W), lambda i, j, t: (i, 0, 2 * nblk + j))],
        out_specs=pl.BlockSpec((1, T, W), lambda i, j, t: (i, t, j)),
        out_shape=jax.ShapeDtypeStruct((b, s, d), F32),
        compiler_params=_params("parallel", "parallel", "arbitrary"),
        name="sb_attention",
    )(qkv, qkv, qkv)


def _sb_layer(x, gain, w_qkv, q_gain, k_gain, w_out):
    b, s, d = x.shape
    x2 = x.reshape(b * s, d)
    q_scale = math.log2(math.e) / math.sqrt(SB_HEAD_DIM)
    head_gain = jnp.concatenate([jnp.tile(q_gain * q_scale, SB_HEADS),
                                 jnp.tile(k_gain, SB_HEADS),
                                 jnp.ones((d,), F32)]).reshape(1, 3 * d)
    qkv = _qkv_proj(x2, gain, w_qkv.astype(BF16), head_gain, 2 * d, 512, 512)
    o = _sb_attention(qkv.reshape(b, s, 3 * d), d, SB_HEAD_PAIRS)
    return _matmul_residual(o.reshape(b * s, d), w_out.astype(BF16), x2, 512).reshape(b, s, d)


FFN_CONFIGS = ((_ffn, 1024, 256), (_ffn, 512, 256), (_ffn_resident, 512, 256),
               (_ffn_resident, 1024, 256))


def kernel(x, mix_norm, pool_in, pool_group, pool_scale, ssd_in, ssd_conv_w, ssd_conv_b, ssd_dt_bias,
           ssd_a_log, ssd_d, ssd_out_norm, ssd_out, sb_qkv, sb_q_norm, sb_k_norm, sb_out, ffn_norm,
           ffn_gate, ffn_up, ffn_down):
    b, s, d = x.shape
    depth = mix_norm.shape[0]
    for i in range(depth):
        kind, j = i % 3, i // 3
        if kind == 0:
            x = _pool_layer(x, mix_norm[i], pool_in[j].astype(BF16), pool_group[j].astype(BF16),
                            pool_scale[j], 512)
        elif kind == 1:
            x = _ssd_layer(x, mix_norm[i], ssd_in[j], ssd_conv_w[j], ssd_conv_b[j], ssd_dt_bias[j],
                           ssd_a_log[j], ssd_d[j], ssd_out_norm[j], ssd_out[j])
        else:
            x = _sb_layer(x, mix_norm[i], sb_qkv[j], sb_q_norm[j], sb_k_norm[j], sb_out[j])
        ffn, tm, th = FFN_CONFIGS[i % len(FFN_CONFIGS)]
        x = ffn(x.reshape(b * s, d), ffn_norm[i], ffn_gate[i].astype(BF16), ffn_up[i].astype(BF16),
                ffn_down[i].astype(BF16), tm, th).reshape(b, s, d)
    return x
```

```python
import functools
import math

import jax
import jax.numpy as jnp
from jax import lax
from jax.experimental import pallas as pl
from jax.experimental.pallas import tpu as pltpu

F32 = jnp.float32
BF16 = jnp.bfloat16

NORM_EPS = 1e-6
POOL_WINDOWS = (2, 4, 8, 16)
POOL_HALO = 16
SSD_HEAD_DIM = 64
SSD_GROUPS = 8
SSD_HEADS_PER_GROUP = 4
SSD_STATE = 128
SSD_CONV = 4
SSD_CHUNK = 256
CONV_HALO = 8
SB_HEADS = 16
SB_HEAD_DIM = 64
SB_TILE = 256
SB_HEAD_PAIRS = 2
SB_EXIT = -152.0
LANES = 128
NEG_BIG = -1e30
TOKEN_TILE = 512
FFN_TOKEN_TILE = 1024
COL_TILE = 512
SSD_COL_TILE = 256
FFN_HIDDEN_TILE = 256

VMEM_LIMIT = 56 * 1024 * 1024


def _params(*sem):
    return pltpu.CompilerParams(dimension_semantics=sem, vmem_limit_bytes=VMEM_LIMIT)


def _resident(shape):
    return pl.BlockSpec(shape, lambda *_: (0,) * len(shape), pipeline_mode=pl.Buffered(1))


def _rms_norm(x, gain):
    return x * lax.rsqrt(jnp.mean(x * x, axis=-1, keepdims=True) + NORM_EPS) * gain


def _silu(x):
    return x / (1.0 + jnp.exp(-x))


def _softplus(x):
    return jnp.maximum(x, 0.0) + jnp.log(1.0 + jnp.exp(-jnp.abs(x)))


def _dot(a, b):
    return jnp.dot(a, b, preferred_element_type=F32)


def _dot_nt(a, b):
    return lax.dot_general(a, b, (((1,), (1,)), ((), ())), preferred_element_type=F32)


def _dot_tn(a, b):
    return lax.dot_general(a, b, (((0,), (0,)), ((), ())), preferred_element_type=F32)


def _split3(a):
    hi = a.astype(BF16)
    r = a - hi.astype(F32)
    mid = r.astype(BF16)
    lo = (r - mid.astype(F32)).astype(BF16)
    return hi, mid, lo


def _matmul_residual_kernel(a_ref, w_ref, r_ref, o_ref):
    o_ref[...] = r_ref[...] + _dot(a_ref[...].astype(BF16), w_ref[...])


def _matmul_residual(a, w, res, tm):
    t, k = a.shape
    n = w.shape[1]
    return pl.pallas_call(
        _matmul_residual_kernel,
        grid=(t // tm,),
        in_specs=[pl.BlockSpec((tm, k), lambda i: (i, 0)),
                  _resident((k, n)),
                  pl.BlockSpec((tm, n), lambda i: (i, 0))],
        out_specs=pl.BlockSpec((tm, n), lambda i: (i, 0)),
        out_shape=jax.ShapeDtypeStruct((t, n), F32),
        compiler_params=_params("parallel"),
        name="matmul_residual",
    )(a, w, res)


def _ffn_kernel(x_ref, g_ref, wg_ref, wu_ref, wd_ref, o_ref, *, th):
    x = x_ref[...]
    h = _rms_norm(x, g_ref[...]).astype(BF16)
    acc = x
    for k in range(wg_ref.shape[1] // th):
        cols = slice(k * th, (k + 1) * th)
        a = _silu(_dot(h, wg_ref[:, cols])) * _dot(h, wu_ref[:, cols])
        acc = acc + _dot(a.astype(BF16), wd_ref[cols, :])
    o_ref[...] = acc


def _ffn(x, gain, wg, wu, wd, tm, th):
    t, d = x.shape
    hid = wg.shape[1]
    return pl.pallas_call(
        functools.partial(_ffn_kernel, th=th),
        grid=(t // tm,),
        in_specs=[pl.BlockSpec((tm, d), lambda i: (i, 0)),
                  _resident((1, d)),
                  _resident((d, hid)),
                  _resident((d, hid)),
                  _resident((hid, d))],
        out_specs=pl.BlockSpec((tm, d), lambda i: (i, 0)),
        out_shape=jax.ShapeDtypeStruct((t, d), F32),
        compiler_params=_params("parallel"),
        name="ffn",
    )(x, gain.reshape(1, d), wg, wu, wd)


def _pool_kernel(x_ref, g_ref, win_ref, wgrp_ref, sc_ref, o_ref, carry_ref, *, ts, gd):
    s = pl.program_id(1)

    @pl.when(s == 0)
    def _():
        carry_ref[...] = jnp.zeros_like(carry_ref)

    x = x_ref[0]
    h = _rms_norm(x, g_ref[...]).astype(BF16)
    u = _dot(h, win_ref[...])
    ext = jnp.concatenate([carry_ref[...], u], axis=0)
    carry_ref[...] = u[ts - POOL_HALO:, :]

    pos = s * ts + lax.broadcasted_iota(jnp.int32, (ts, 1), 0)
    for g, w in enumerate(POOL_WINDOWS):
        cols = slice(g * gd, (g + 1) * gd)
        acc = ext[:, cols]
        span = 1
        while span < w:
            acc = acc + pltpu.roll(acc, span, axis=0)
            span *= 2
        inv_count = 1.0 / jnp.minimum(pos + 1, w).astype(F32)
        p = acc[POOL_HALO:, :] * inv_count - u[:, cols]
        y = _dot(p.astype(BF16), wgrp_ref[g])
        o_ref[0, :, cols] = x[:, cols] + y * sc_ref[:, cols]


def _pool_layer(x, gain, w_in, w_group, scale, ts):
    b, s, d = x.shape
    ng, gd, _ = w_group.shape
    kern = functools.partial(_pool_kernel, ts=ts, gd=gd)
    return pl.pallas_call(
        kern,
        grid=(b, s // ts),
        in_specs=[pl.BlockSpec((1, ts, d), lambda i, j: (i, j, 0)),
                  _resident((1, d)),
                  _resident((d, d)),
                  _resident((ng, gd, gd)),
                  _resident((1, d))],
        out_specs=pl.BlockSpec((1, ts, d), lambda i, j: (i, j, 0)),
        out_shape=jax.ShapeDtypeStruct((b, s, d), F32),
        scratch_shapes=[pltpu.VMEM((POOL_HALO, d), F32)],
        compiler_params=_params("parallel", "arbitrary"),
        name="pool_mixer",
    )(x, gain.reshape(1, d), w_in, w_group, scale.reshape(1, d))


def _ssd_in_kernel(x_ref, g_ref, wz_ref, wxbc_ref, wdt_ref, wdtt_ref, cw_ref, cb_ref, dtb_r_ref,
                   dtb_c_ref, z_ref, xbc_ref, dtc_ref, dtr_ref, halo_ref, *, tiles_per_seq, tn):
    assert SSD_CONV == 4
    i = pl.program_id(0)
    tm = x_ref.shape[0]
    h = _rms_norm(x_ref[...], g_ref[...]).astype(BF16)

    for c in range(wz_ref.shape[1] // tn):
        cols = slice(c * tn, (c + 1) * tn)
        z_ref[:, cols] = _dot(h, wz_ref[:, cols]).astype(z_ref.dtype)

    dtc_ref[...] = _softplus(_dot(h, wdt_ref[...]) + dtb_r_ref[...])
    dtr_ref[...] = _softplus(_dot_nt(wdtt_ref[...], h) + dtb_c_ref[...])

    @pl.when(i % tiles_per_seq == 0)
    def _():
        halo_ref[...] = jnp.zeros_like(halo_ref)

    for c in range(wxbc_ref.shape[1] // tn):
        cols = slice(c * tn, (c + 1) * tn)
        raw = _dot(h, wxbc_ref[:, cols])
        ext = jnp.concatenate([halo_ref[:, cols], raw], axis=0)
        halo_ref[:, cols] = raw[tm - CONV_HALO:, :]
        prev = pltpu.roll(ext, 1, axis=0)
        p = ext * cw_ref[3:4, cols] + prev * cw_ref[2:3, cols]
        q = ext * cw_ref[1:2, cols] + prev * cw_ref[0:1, cols]
        acc = (p + pltpu.roll(q, 2, axis=0))[CONV_HALO:, :]
        xbc_ref[:, cols] = _silu(acc + cb_ref[:, cols]).astype(xbc_ref.dtype)


def _ssd_in_proj(x, gain, w_z, w_xbc, w_dt, conv_w, conv_b, dt_bias, tiles_per_seq, tm, tn):
    t, d = x.shape
    nz, nxbc, nh = w_z.shape[1], w_xbc.shape[1], w_dt.shape[1]
    kern = functools.partial(_ssd_in_kernel, tiles_per_seq=tiles_per_seq, tn=tn)
    return pl.pallas_call(
        kern,
        grid=(t // tm,),
        in_specs=[pl.BlockSpec((tm, d), lambda i: (i, 0)),
                  _resident((1, d)),
                  _resident((d, nz)),
                  _resident((d, nxbc)),
                  _resident((d, nh)),
                  _resident((nh, d)),
                  _resident((SSD_CONV, nxbc)),
                  _resident((1, nxbc)),
                  _resident((1, nh)),
                  _resident((nh, 1))],
        out_specs=[pl.BlockSpec((tm, nz), lambda i: (i, 0)),
                   pl.BlockSpec((tm, nxbc), lambda i: (i, 0)),
                   pl.BlockSpec((tm, nh), lambda i: (i, 0)),
                   pl.BlockSpec((nh, tm), lambda i: (0, i))],
        out_shape=[jax.ShapeDtypeStruct((t, nz), BF16),
                   jax.ShapeDtypeStruct((t, nxbc), BF16),
                   jax.ShapeDtypeStruct((t, nh), F32),
                   jax.ShapeDtypeStruct((nh, t), F32)],
        scratch_shapes=[pltpu.VMEM((CONV_HALO, nxbc), F32)],
        compiler_params=_params("arbitrary"),
        name="ssd_in_proj",
    )(x, gain.reshape(1, d), w_z, w_xbc, w_dt, w_dt.T, conv_w, conv_b.reshape(1, nxbc),
      dt_bias.reshape(1, nh), dt_bias.reshape(nh, 1))


def _ssd_kernel(z_ref, xbc_ref, dtc_ref, dtr_ref, res_ref, alog_r_ref, alog_c_ref, dsk_ref,
                onorm_ref, wout_ref, o_ref, state_ref):
    L, P, HG, G, N = SSD_CHUNK, SSD_HEAD_DIM, SSD_HEADS_PER_GROUP, SSD_GROUPS, SSD_STATE
    GW = HG * P
    boff, coff = G * GW, G * GW + G * N

    @pl.when(pl.program_id(1) == 0)
    def _():
        state_ref[...] = jnp.zeros_like(state_ref)

    dt_c = dtc_ref[0]
    da_c = dt_c * (-jnp.exp(alog_r_ref[...]))
    da_r = dtr_ref[...] * (-jnp.exp(alog_c_ref[...]))
    row = lax.broadcasted_iota(jnp.int32, (L, L), 0)
    col = lax.broadcasted_iota(jnp.int32, (L, L), 1)
    causal = row >= col
    tri = jnp.where(causal, 1.0, 0.0).astype(BF16)
    tri_t = jnp.where(row <= col, 1.0, 0.0).astype(BF16)
    acum_c = sum(_dot(tri, piece) for piece in _split3(da_c))
    acum_r = sum(_dot(piece, tri_t) for piece in _split3(da_r))
    a_last = acum_c[L - 1:L, :]

    nh = dt_c.shape[1]
    head_of_col = lax.shift_right_logical(lax.broadcasted_iota(jnp.int32, (nh, nh * P), 1),
                                          int(math.log2(P)))
    expand = jnp.where(head_of_col == lax.broadcasted_iota(jnp.int32, (nh, nh * P), 0),
                       1.0, 0.0).astype(BF16)
    dt_w = _dot(dt_c.astype(BF16), expand)
    tail_w = _dot((dt_c * jnp.exp(a_last - acum_c)).astype(BF16), expand)
    grow_hi, grow_mid, _ = _split3(jnp.exp(acum_c))
    grow_w = _dot(grow_hi, expand) + _dot(grow_mid, expand)
    head_of_lane = lax.shift_right_logical(lax.broadcasted_iota(jnp.int32, (1, GW), 1),
                                           int(math.log2(P)))

    acc = res_ref[0]
    for g in range(G):
        gcols = slice(g * GW, (g + 1) * GW)
        xs = xbc_ref[0, :, gcols].astype(F32)
        bm = xbc_ref[0, :, boff + g * N:boff + (g + 1) * N]
        cm = xbc_ref[0, :, coff + g * N:coff + (g + 1) * N]
        st = state_ref[g]
        cb = _dot_nt(cm, bm)
        xdt = (xs * dt_w[:, gcols]).astype(BF16)
        y = _dot(cm, st.astype(BF16)) * grow_w[:, gcols] + dsk_ref[:, gcols] * xs

        for hh in range(HG):
            hd = g * HG + hh
            decay = jnp.exp(jnp.where(causal, acum_c[:, hd:hd + 1] - acum_r[hd:hd + 1, :], NEG_BIG))
            y = y + _dot((cb * decay).astype(BF16),
                         jnp.where(head_of_lane == hh, xdt, jnp.zeros_like(xdt)))

        xw = (xs * tail_w[:, gcols]).astype(BF16)
        state_ref[g] = st * grow_w[L - 1:L, gcols] + _dot_tn(bm, xw)

        gated = y * _silu(z_ref[0, :, gcols].astype(F32))
        gated = gated * lax.rsqrt(jnp.mean(gated * gated, axis=-1, keepdims=True) + NORM_EPS)
        gated = gated * onorm_ref[:, g * GW:(g + 1) * GW]
        acc = acc + _dot(gated.astype(BF16), wout_ref[g * GW:(g + 1) * GW, :])

    o_ref[0] = acc


def _ssd_layer(x, gain, w_in, conv_w, conv_b, dt_bias, a_log, d_skip, out_norm, w_out):
    b, s, d = x.shape
    L, G, N = SSD_CHUNK, SSD_GROUPS, SSD_STATE
    GW = SSD_HEADS_PER_GROUP * SSD_HEAD_DIM
    d_inner = G * GW
    nxbc = d_inner + 2 * G * N
    nh = G * SSD_HEADS_PER_GROUP
    nchunks = s // L
    w_z = w_in[:, :d_inner].astype(BF16)
    w_xbc = w_in[:, d_inner:d_inner + nxbc].astype(BF16)
    w_dt = w_in[:, d_inner + nxbc:].astype(BF16)
    z, xbc, dt_c, dt_r = _ssd_in_proj(x.reshape(b * s, d), gain, w_z, w_xbc, w_dt, conv_w, conv_b,
                                      dt_bias, s // TOKEN_TILE, TOKEN_TILE, SSD_COL_TILE)
    return pl.pallas_call(
        _ssd_kernel,
        grid=(b, nchunks),
        in_specs=[pl.BlockSpec((1, L, d_inner), lambda i, c: (i, c, 0)),
                  pl.BlockSpec((1, L, nxbc), lambda i, c: (i, c, 0)),
                  pl.BlockSpec((1, L, nh), lambda i, c: (i, c, 0)),
                  pl.BlockSpec((nh, L), lambda i, c: (0, i * nchunks + c)),
                  pl.BlockSpec((1, L, d), lambda i, c: (i, c, 0)),
                  _resident((1, nh)),
                  _resident((nh, 1)),
                  _resident((1, d_inner)),
                  _resident((1, d_inner)),
                  _resident((d_inner, d))],
        out_specs=pl.BlockSpec((1, L, d), lambda i, c: (i, c, 0)),
        out_shape=jax.ShapeDtypeStruct((b, s, d), F32),
        scratch_shapes=[pltpu.VMEM((G, N, GW), F32)],
        compiler_params=_params("parallel", "arbitrary"),
        name="ssd_scan",
    )(z.reshape(b, s, d_inner), xbc.reshape(b, s, nxbc), dt_c.reshape(b, s, nh), dt_r, x,
      a_log.reshape(1, nh), a_log.reshape(nh, 1), jnp.repeat(d_skip, SSD_HEAD_DIM).reshape(1, d_inner),
      out_norm.reshape(1, d_inner), w_out.astype(BF16))


def _first_head_lanes():
    return lax.broadcasted_iota(jnp.int32, (1, LANES), 1) < SB_HEAD_DIM


def _qkv_kernel(x_ref, g_ref, w_ref, hg_ref, o_ref, *, n_norm_cols, tn):
    h = _rms_norm(x_ref[...], g_ref[...]).astype(BF16)
    first = _first_head_lanes()
    for j in range(w_ref.shape[1] // tn):
        y = _dot(h, w_ref[:, j * tn:(j + 1) * tn])
        if j * tn >= n_norm_cols:
            o_ref[:, j * tn:(j + 1) * tn] = y.astype(o_ref.dtype)
            continue
        for c in range(tn // LANES):
            cols = slice(j * tn + c * LANES, j * tn + (c + 1) * LANES)
            blk = y[:, c * LANES:(c + 1) * LANES]
            sq = blk * blk
            ms0 = jnp.sum(jnp.where(first, sq, 0.0), axis=-1, keepdims=True) * (1.0 / SB_HEAD_DIM)
            ms1 = jnp.sum(jnp.where(first, 0.0, sq), axis=-1, keepdims=True) * (1.0 / SB_HEAD_DIM)
            inv = jnp.where(first, lax.rsqrt(ms0 + NORM_EPS), lax.rsqrt(ms1 + NORM_EPS))
            o_ref[:, cols] = (blk * inv * hg_ref[:, cols]).astype(o_ref.dtype)


def _qkv_proj(x, gain, w, head_gain, n_norm_cols, tm, tn):
    t, d = x.shape
    n = w.shape[1]
    kern = functools.partial(_qkv_kernel, n_norm_cols=n_norm_cols, tn=tn)
    return pl.pallas_call(
        kern,
        grid=(t // tm,),
        in_specs=[pl.BlockSpec((tm, d), lambda i: (i, 0)),
                  _resident((1, d)),
                  _resident((d, n)),
                  _resident((1, n_norm_cols))],
        out_specs=pl.BlockSpec((tm, n), lambda i: (i, 0)),
        out_shape=jax.ShapeDtypeStruct((t, n), BF16),
        compiler_params=_params("parallel"),
        name="sb_qkv_proj",
    )(x, gain.reshape(1, d), w, head_gain)


def _sb_kernel(q_ref, k_ref, v_ref, o_ref, *, hp):
    qi = pl.program_id(2)
    T = SB_TILE
    first = _first_head_lanes()
    row = lax.broadcasted_iota(jnp.int32, (T, T), 0)
    col = lax.broadcasted_iota(jnp.int32, (T, T), 1)
    strict = row > col
    later = jnp.where(strict, 1.0, 0.0).astype(BF16)

    qs = []
    for c in range(hp):
        qc = q_ref[0, :, c * LANES:(c + 1) * LANES]
        qs.append(jnp.where(first, qc, jnp.zeros_like(qc)))
        qs.append(jnp.where(first, jnp.zeros_like(qc), qc))

    def tile(qh, kb, vb, run, diagonal):
        z = _dot_nt(qh, kb)
        sp = jnp.maximum(z, 0.0) + jnp.log2(1.0 + jnp.exp2(-jnp.abs(z)))
        log_1m = -sp
        if diagonal:
            log_1m = jnp.where(strict, log_1m, 0.0)
        after = run + _dot(log_1m.astype(BF16), later)
        a = jnp.exp2(z - sp + after)
        if diagonal:
            a = jnp.where(strict, a, 0.0)
        pv = _dot(a.astype(BF16), vb)
        return pv, run + jnp.sum(log_1m, axis=1, keepdims=True)

    def step(j, os, runs, diagonal):
        start = pl.multiple_of(j * T, T)
        new_os, new_runs = [], []
        for c in range(hp):
            kb = k_ref[0, pl.ds(start, T), c * LANES:(c + 1) * LANES]
            vb = v_ref[0, pl.ds(start, T), c * LANES:(c + 1) * LANES]
            pa, ra = tile(qs[2 * c], kb, vb, runs[2 * c], diagonal)
            pb, rb = tile(qs[2 * c + 1], kb, vb, runs[2 * c + 1], diagonal)
            new_os.append(os[c] + jnp.where(first, pa, pb))
            new_runs += [ra, rb]
        return new_os, new_runs

    def unfinished(runs):
        worst = functools.reduce(jnp.maximum, runs)
        return (jnp.max(worst) >= SB_EXIT).astype(jnp.int32)

    os = [jnp.zeros((T, LANES), F32) for _ in range(hp)]
    runs = [jnp.zeros((T, 1), F32) for _ in range(2 * hp)]
    os, runs = step(qi, os, runs, True)

    def cond(carry):
        j, go = carry[0], carry[1]
        return jnp.logical_and(j >= 0, go > 0)

    def body(carry):
        j = carry[0]
        os, runs = step(j, list(carry[2:2 + hp]), list(carry[2 + hp:]), False)
        return (j - 1, unfinished(runs), *os, *runs)

    final = lax.while_loop(cond, body, (qi - 1, unfinished(runs), *os, *runs))
    for c in range(hp):
        o_ref[0, :, c * LANES:(c + 1) * LANES] = final[2 + c]


def _sb_attention(qkv, d, hp):
    b, s, _ = qkv.shape
    T = SB_TILE
    W = LANES * hp
    nblk = d // W
    kern = functools.partial(_sb_kernel, hp=hp)
    return pl.pallas_call(
        kern,
        grid=(b, nblk, s // T),
        in_specs=[pl.BlockSpec((1, T, W), lambda i, j, t: (i, t, j)),
                  pl.BlockSpec((1, s, W), lambda i, j, t: (i, 0, nblk + j)),
                  pl.BlockSpec((1, s, W), lambda i, j, t: (i, 0, 2 * nblk + j))],
        out_specs=pl.BlockSpec((1, T, W), lambda i, j, t: (i, t, j)),
        out_shape=jax.ShapeDtypeStruct((b, s, d), F32),
        compiler_params=_params("parallel", "parallel", "arbitrary"),
        name="sb_attention",
    )(qkv, qkv, qkv)


def _sb_layer(x, gain, w_qkv, q_gain, k_gain, w_out):
    b, s, d = x.shape
    x2 = x.reshape(b * s, d)
    q_scale = math.log2(math.e) / math.sqrt(SB_HEAD_DIM)
    head_gain = jnp.concatenate([jnp.tile(q_gain * q_scale, SB_HEADS),
                                 jnp.tile(k_gain, SB_HEADS)]).reshape(1, 2 * d)
    qkv = _qkv_proj(x2, gain, w_qkv.astype(BF16), head_gain, 2 * d, TOKEN_TILE, COL_TILE)
    o = _sb_attention(qkv.reshape(b, s, 3 * d), d, SB_HEAD_PAIRS)
    return _matmul_residual(o.reshape(b * s, d), w_out.astype(BF16), x2, TOKEN_TILE).reshape(b, s, d)


def kernel(x, mix_norm, pool_in, pool_group, pool_scale, ssd_in, ssd_conv_w, ssd_conv_b, ssd_dt_bias,
           ssd_a_log, ssd_d, ssd_out_norm, ssd_out, sb_qkv, sb_q_norm, sb_k_norm, sb_out, ffn_norm,
           ffn_gate, ffn_up, ffn_down):
    b, s, d = x.shape
    depth = mix_norm.shape[0]
    for i in range(depth):
        kind, j = i % 3, i // 3
        if kind == 0:
            x = _pool_layer(x, mix_norm[i], pool_in[j].astype(BF16), pool_group[j].astype(BF16),
                            pool_scale[j], TOKEN_TILE)
        elif kind == 1:
            x = _ssd_layer(x, mix_norm[i], ssd_in[j], ssd_conv_w[j], ssd_conv_b[j], ssd_dt_bias[j],
                           ssd_a_log[j], ssd_d[j], ssd_out_norm[j], ssd_out[j])
        else:
            x = _sb_layer(x, mix_norm[i], sb_qkv[j], sb_q_norm[j], sb_k_norm[j], sb_out[j])
        x = _ffn(x.reshape(b * s, d), ffn_norm[i], ffn_gate[i].astype(BF16), ffn_up[i].astype(BF16),
                 ffn_down[i].astype(BF16), FFN_TOKEN_TILE, FFN_HIDDEN_TILE).reshape(b, s, d)
    return x
```

```python
import functools
import math

import jax
import jax.numpy as jnp
from jax import lax
from jax.experimental import pallas as pl
from jax.experimental.pallas import tpu as pltpu

F32 = jnp.float32
BF16 = jnp.bfloat16

NORM_EPS = 1e-6
POOL_WINDOWS = (2, 4, 8, 16)
POOL_HALO = 16
SSD_HEAD_DIM = 64
SSD_GROUPS = 8
SSD_HEADS_PER_GROUP = 4
SSD_STATE = 128
SSD_CONV = 4
SSD_CHUNK = 256
CONV_HALO = 8
SB_HEADS = 16
SB_HEAD_DIM = 64
SB_TILE = 256
SB_HEAD_PAIRS = 4
SB_EXIT = -152.0
LANES = 128
NEG_BIG = -1e30
TOKEN_TILE = 512
FFN_TOKEN_TILE = 1024
COL_TILE = 512
SSD_COL_TILE = 256
FFN_HIDDEN_TILE = 256

VMEM_LIMIT = 56 * 1024 * 1024


def _params(*sem):
    return pltpu.CompilerParams(dimension_semantics=sem, vmem_limit_bytes=VMEM_LIMIT)


def _resident(shape):
    return pl.BlockSpec(shape, lambda *_: (0,) * len(shape), pipeline_mode=pl.Buffered(1))


def _rms_norm(x, gain):
    return x * lax.rsqrt(jnp.mean(x * x, axis=-1, keepdims=True) + NORM_EPS) * gain


def _silu(x):
    return x / (1.0 + jnp.exp(-x))


def _softplus(x):
    return jnp.maximum(x, 0.0) + jnp.log(1.0 + jnp.exp(-jnp.abs(x)))


def _dot(a, b):
    return jnp.dot(a, b, preferred_element_type=F32)


def _dot_nt(a, b):
    return lax.dot_general(a, b, (((1,), (1,)), ((), ())), preferred_element_type=F32)


def _dot_tn(a, b):
    return lax.dot_general(a, b, (((0,), (0,)), ((), ())), preferred_element_type=F32)


def _split3(a):
    hi = a.astype(BF16)
    r = a - hi.astype(F32)
    mid = r.astype(BF16)
    lo = (r - mid.astype(F32)).astype(BF16)
    return hi, mid, lo


def _matmul_residual_kernel(a_ref, w_ref, r_ref, o_ref):
    o_ref[...] = r_ref[...] + _dot(a_ref[...].astype(BF16), w_ref[...])


def _matmul_residual(a, w, res, tm):
    t, k = a.shape
    n = w.shape[1]
    return pl.pallas_call(
        _matmul_residual_kernel,
        grid=(t // tm,),
        in_specs=[pl.BlockSpec((tm, k), lambda i: (i, 0)),
                  _resident((k, n)),
                  pl.BlockSpec((tm, n), lambda i: (i, 0))],
        out_specs=pl.BlockSpec((tm, n), lambda i: (i, 0)),
        out_shape=jax.ShapeDtypeStruct((t, n), F32),
        compiler_params=_params("parallel"),
        name="matmul_residual",
    )(a, w, res)


def _ffn_kernel(x_ref, g_ref, wg_ref, wu_ref, wd_ref, o_ref, *, th):
    x = x_ref[...]
    h = _rms_norm(x, g_ref[...]).astype(BF16)
    acc = x
    for k in range(wg_ref.shape[1] // th):
        cols = slice(k * th, (k + 1) * th)
        a = _silu(_dot(h, wg_ref[:, cols])) * _dot(h, wu_ref[:, cols])
        acc = acc + _dot(a.astype(BF16), wd_ref[cols, :])
    o_ref[...] = acc


def _ffn(x, gain, wg, wu, wd, tm, th):
    t, d = x.shape
    hid = wg.shape[1]
    return pl.pallas_call(
        functools.partial(_ffn_kernel, th=th),
        grid=(t // tm,),
        in_specs=[pl.BlockSpec((tm, d), lambda i: (i, 0)),
                  _resident((1, d)),
                  _resident((d, hid)),
                  _resident((d, hid)),
                  _resident((hid, d))],
        out_specs=pl.BlockSpec((tm, d), lambda i: (i, 0)),
        out_shape=jax.ShapeDtypeStruct((t, d), F32),
        compiler_params=_params("parallel"),
        name="ffn",
    )(x, gain.reshape(1, d), wg, wu, wd)


def _pool_kernel(x_ref, g_ref, win_ref, wgrp_ref, sc_ref, o_ref, carry_ref, *, ts, gd):
    s = pl.program_id(1)

    @pl.when(s == 0)
    def _():
        carry_ref[...] = jnp.zeros_like(carry_ref)

    x = x_ref[0]
    h = _rms_norm(x, g_ref[...]).astype(BF16)
    u = _dot(h, win_ref[...])
    ext = jnp.concatenate([carry_ref[...], u], axis=0)
    carry_ref[...] = u[ts - POOL_HALO:, :]

    pos = s * ts + lax.broadcasted_iota(jnp.int32, (ts, 1), 0)
    for g, w in enumerate(POOL_WINDOWS):
        cols = slice(g * gd, (g + 1) * gd)
        acc = ext[:, cols]
        span = 1
        while span < w:
            acc = acc + pltpu.roll(acc, span, axis=0)
            span *= 2
        inv_count = 1.0 / jnp.minimum(pos + 1, w).astype(F32)
        p = acc[POOL_HALO:, :] * inv_count - u[:, cols]
        y = _dot(p.astype(BF16), wgrp_ref[g])
        o_ref[0, :, cols] = x[:, cols] + y * sc_ref[:, cols]


def _pool_layer(x, gain, w_in, w_group, scale, ts):
    b, s, d = x.shape
    ng, gd, _ = w_group.shape
    kern = functools.partial(_pool_kernel, ts=ts, gd=gd)
    return pl.pallas_call(
        kern,
        grid=(b, s // ts),
        in_specs=[pl.BlockSpec((1, ts, d), lambda i, j: (i, j, 0)),
                  _resident((1, d)),
                  _resident((d, d)),
                  _resident((ng, gd, gd)),
                  _resident((1, d))],
        out_specs=pl.BlockSpec((1, ts, d), lambda i, j: (i, j, 0)),
        out_shape=jax.ShapeDtypeStruct((b, s, d), F32),
        scratch_shapes=[pltpu.VMEM((POOL_HALO, d), F32)],
        compiler_params=_params("parallel", "arbitrary"),
        name="pool_mixer",
    )(x, gain.reshape(1, d), w_in, w_group, scale.reshape(1, d))


def _ssd_in_kernel(x_ref, g_ref, wz_ref, wxbc_ref, wdt_ref, wdtt_ref, cw_ref, cb_ref, dtb_r_ref,
                   dtb_c_ref, z_ref, xbc_ref, dtc_ref, dtr_ref, halo_ref, *, tiles_per_seq, tn):
    assert SSD_CONV == 4
    i = pl.program_id(0)
    tm = x_ref.shape[0]
    h = _rms_norm(x_ref[...], g_ref[...]).astype(BF16)

    for c in range(wz_ref.shape[1] // tn):
        cols = slice(c * tn, (c + 1) * tn)
        z_ref[:, cols] = _dot(h, wz_ref[:, cols]).astype(z_ref.dtype)

    dtc_ref[...] = _softplus(_dot(h, wdt_ref[...]) + dtb_r_ref[...])
    dtr_ref[...] = _softplus(_dot_nt(wdtt_ref[...], h) + dtb_c_ref[...])

    @pl.when(i % tiles_per_seq == 0)
    def _():
        halo_ref[...] = jnp.zeros_like(halo_ref)

    for c in range(wxbc_ref.shape[1] // tn):
        cols = slice(c * tn, (c + 1) * tn)
        raw = _dot(h, wxbc_ref[:, cols])
        ext = jnp.concatenate([halo_ref[:, cols], raw], axis=0)
        halo_ref[:, cols] = raw[tm - CONV_HALO:, :]
        prev = pltpu.roll(ext, 1, axis=0)
        p = ext * cw_ref[3:4, cols] + prev * cw_ref[2:3, cols]
        q = ext * cw_ref[1:2, cols] + prev * cw_ref[0:1, cols]
        acc = (p + pltpu.roll(q, 2, axis=0))[CONV_HALO:, :]
        xbc_ref[:, cols] = _silu(acc + cb_ref[:, cols]).astype(xbc_ref.dtype)


def _ssd_in_proj(x, gain, w_z, w_xbc, w_dt, conv_w, conv_b, dt_bias, tiles_per_seq, tm, tn):
    t, d = x.shape
    nz, nxbc, nh = w_z.shape[1], w_xbc.shape[1], w_dt.shape[1]
    kern = functools.partial(_ssd_in_kernel, tiles_per_seq=tiles_per_seq, tn=tn)
    return pl.pallas_call(
        kern,
        grid=(t // tm,),
        in_specs=[pl.BlockSpec((tm, d), lambda i: (i, 0)),
                  _resident((1, d)),
                  _resident((d, nz)),
                  _resident((d, nxbc)),
                  _resident((d, nh)),
                  _resident((nh, d)),
                  _resident((SSD_CONV, nxbc)),
                  _resident((1, nxbc)),
                  _resident((1, nh)),
                  _resident((nh, 1))],
        out_specs=[pl.BlockSpec((tm, nz), lambda i: (i, 0)),
                   pl.BlockSpec((tm, nxbc), lambda i: (i, 0)),
                   pl.BlockSpec((tm, nh), lambda i: (i, 0)),
                   pl.BlockSpec((nh, tm), lambda i: (0, i))],
        out_shape=[jax.ShapeDtypeStruct((t, nz), BF16),
                   jax.ShapeDtypeStruct((t, nxbc), BF16),
                   jax.ShapeDtypeStruct((t, nh), F32),
                   jax.ShapeDtypeStruct((nh, t), F32)],
        scratch_shapes=[pltpu.VMEM((CONV_HALO, nxbc), F32)],
        compiler_params=_params("arbitrary"),
        name="ssd_in_proj",
    )(x, gain.reshape(1, d), w_z, w_xbc, w_dt, w_dt.T, conv_w, conv_b.reshape(1, nxbc),
      dt_bias.reshape(1, nh), dt_bias.reshape(nh, 1))


def _ssd_kernel(z_ref, xbc_ref, dtc_ref, dtr_ref, res_ref, alog_r_ref, alog_c_ref, dsk_ref,
                onorm_ref, wout_ref, o_ref, state_ref, gated_ref):
    L, P, HG, G, N = SSD_CHUNK, SSD_HEAD_DIM, SSD_HEADS_PER_GROUP, SSD_GROUPS, SSD_STATE
    GW = HG * P
    boff, coff = G * GW, G * GW + G * N

    @pl.when(pl.program_id(1) == 0)
    def _():
        state_ref[...] = jnp.zeros_like(state_ref)

    dt_c = dtc_ref[0]
    da_c = dt_c * (-jnp.exp(alog_r_ref[...]))
    da_r = dtr_ref[...] * (-jnp.exp(alog_c_ref[...]))
    row = lax.broadcasted_iota(jnp.int32, (L, L), 0)
    col = lax.broadcasted_iota(jnp.int32, (L, L), 1)
    causal = row >= col
    tri = jnp.where(causal, 1.0, 0.0).astype(BF16)
    tri_t = jnp.where(row <= col, 1.0, 0.0).astype(BF16)
    acum_c = sum(_dot(tri, piece) for piece in _split3(da_c))
    acum_r = sum(_dot(piece, tri_t) for piece in _split3(da_r))
    a_last = acum_c[L - 1:L, :]

    nh = dt_c.shape[1]
    head_of_col = lax.shift_right_logical(lax.broadcasted_iota(jnp.int32, (nh, nh * P), 1),
                                          int(math.log2(P)))
    expand = jnp.where(head_of_col == lax.broadcasted_iota(jnp.int32, (nh, nh * P), 0),
                       1.0, 0.0).astype(BF16)
    dt_w = _dot(dt_c.astype(BF16), expand)
    tail_w = _dot((dt_c * jnp.exp(a_last - acum_c)).astype(BF16), expand)
    grow_hi, grow_mid, _ = _split3(jnp.exp(acum_c))
    grow_w = _dot(grow_hi, expand) + _dot(grow_mid, expand)
    head_of_lane = lax.shift_right_logical(lax.broadcasted_iota(jnp.int32, (1, GW), 1),
                                           int(math.log2(P)))

    groups = range(G)
    gcols = [slice(g * GW, (g + 1) * GW) for g in groups]
    xs = [xbc_ref[0, :, gcols[g]].astype(F32) for g in groups]
    bm = [xbc_ref[0, :, boff + g * N:boff + (g + 1) * N] for g in groups]
    cm = [xbc_ref[0, :, coff + g * N:coff + (g + 1) * N] for g in groups]
    st = [state_ref[g] for g in groups]
    cb = [_dot_nt(cm[g], bm[g]) for g in groups]
    carried = [_dot(cm[g], st[g].astype(BF16)) for g in groups]
    xdt = [(xs[g] * dt_w[:, gcols[g]]).astype(BF16) for g in groups]

    for g in groups:
        xw = (xs[g] * tail_w[:, gcols[g]]).astype(BF16)
        state_ref[g] = st[g] * grow_w[L - 1:L, gcols[g]] + _dot_tn(bm[g], xw)

    for g in groups:
        y = carried[g] * grow_w[:, gcols[g]] + dsk_ref[:, gcols[g]] * xs[g]
        for hh in range(HG):
            hd = g * HG + hh
            decay = jnp.exp(jnp.where(causal, acum_c[:, hd:hd + 1] - acum_r[hd:hd + 1, :], NEG_BIG))
            y = y + _dot((cb[g] * decay).astype(BF16),
                         jnp.where(head_of_lane == hh, xdt[g], jnp.zeros_like(xdt[g])))
        gated = y * _silu(z_ref[0, :, gcols[g]].astype(F32))
        gated = gated * lax.rsqrt(jnp.mean(gated * gated, axis=-1, keepdims=True) + NORM_EPS)
        gated_ref[:, gcols[g]] = (gated * onorm_ref[:, gcols[g]]).astype(BF16)

    o_ref[0] = res_ref[0] + _dot(gated_ref[...], wout_ref[...])


def _ssd_layer(x, gain, w_in, conv_w, conv_b, dt_bias, a_log, d_skip, out_norm, w_out):
    b, s, d = x.shape
    L, G, N = SSD_CHUNK, SSD_GROUPS, SSD_STATE
    GW = SSD_HEADS_PER_GROUP * SSD_HEAD_DIM
    d_inner = G * GW
    nxbc = d_inner + 2 * G * N
    nh = G * SSD_HEADS_PER_GROUP
    nchunks = s // L
    w_z = w_in[:, :d_inner].astype(BF16)
    w_xbc = w_in[:, d_inner:d_inner + nxbc].astype(BF16)
    w_dt = w_in[:, d_inner + nxbc:].astype(BF16)
    z, xbc, dt_c, dt_r = _ssd_in_proj(x.reshape(b * s, d), gain, w_z, w_xbc, w_dt, conv_w, conv_b,
                                      dt_bias, s // TOKEN_TILE, TOKEN_TILE, SSD_COL_TILE)
    return pl.pallas_call(
        _ssd_kernel,
        grid=(b, nchunks),
        in_specs=[pl.BlockSpec((1, L, d_inner), lambda i, c: (i, c, 0)),
                  pl.BlockSpec((1, L, nxbc), lambda i, c: (i, c, 0)),
                  pl.BlockSpec((1, L, nh), lambda i, c: (i, c, 0)),
                  pl.BlockSpec((nh, L), lambda i, c: (0, i * nchunks + c)),
                  pl.BlockSpec((1, L, d), lambda i, c: (i, c, 0)),
                  _resident((1, nh)),
                  _resident((nh, 1)),
                  _resident((1, d_inner)),
                  _resident((1, d_inner)),
                  _resident((d_inner, d))],
        out_specs=pl.BlockSpec((1, L, d), lambda i, c: (i, c, 0)),
        out_shape=jax.ShapeDtypeStruct((b, s, d), F32),
        scratch_shapes=[pltpu.VMEM((G, N, GW), F32),
                        pltpu.VMEM((L, d_inner), BF16)],
        compiler_params=_params("parallel", "arbitrary"),
        name="ssd_scan",
    )(z.reshape(b, s, d_inner), xbc.reshape(b, s, nxbc), dt_c.reshape(b, s, nh), dt_r, x,
      a_log.reshape(1, nh), a_log.reshape(nh, 1), jnp.repeat(d_skip, SSD_HEAD_DIM).reshape(1, d_inner),
      out_norm.reshape(1, d_inner), w_out.astype(BF16))


def _first_head_lanes():
    return lax.broadcasted_iota(jnp.int32, (1, LANES), 1) < SB_HEAD_DIM


def _qkv_kernel(x_ref, g_ref, w_ref, hg_ref, o_ref, *, n_norm_cols, tn):
    h = _rms_norm(x_ref[...], g_ref[...]).astype(BF16)
    first = _first_head_lanes()
    for j in range(w_ref.shape[1] // tn):
        y = _dot(h, w_ref[:, j * tn:(j + 1) * tn])
        if j * tn >= n_norm_cols:
            o_ref[:, j * tn:(j + 1) * tn] = y.astype(o_ref.dtype)
            continue
        for c in range(tn // LANES):
            cols = slice(j * tn + c * LANES, j * tn + (c + 1) * LANES)
            blk = y[:, c * LANES:(c + 1) * LANES]
            sq = blk * blk
            ms0 = jnp.sum(jnp.where(first, sq, 0.0), axis=-1, keepdims=True) * (1.0 / SB_HEAD_DIM)
            ms1 = jnp.sum(jnp.where(first, 0.0, sq), axis=-1, keepdims=True) * (1.0 / SB_HEAD_DIM)
            inv = jnp.where(first, lax.rsqrt(ms0 + NORM_EPS), lax.rsqrt(ms1 + NORM_EPS))
            o_ref[:, cols] = (blk * inv * hg_ref[:, cols]).astype(o_ref.dtype)


def _qkv_proj(x, gain, w, head_gain, n_norm_cols, tm, tn):
    t, d = x.shape
    n = w.shape[1]
    kern = functools.partial(_qkv_kernel, n_norm_cols=n_norm_cols, tn=tn)
    return pl.pallas_call(
        kern,
        grid=(t // tm,),
        in_specs=[pl.BlockSpec((tm, d), lambda i: (i, 0)),
                  _resident((1, d)),
                  _resident((d, n)),
                  _resident((1, n_norm_cols))],
        out_specs=pl.BlockSpec((tm, n), lambda i: (i, 0)),
        out_shape=jax.ShapeDtypeStruct((t, n), BF16),
        compiler_params=_params("parallel"),
        name="sb_qkv_proj",
    )(x, gain.reshape(1, d), w, head_gain)


def _sb_kernel(q_ref, k_ref, v_ref, o_ref, run_ref, *, hp):
    qi = pl.program_id(2)
    T = SB_TILE
    first = _first_head_lanes()
    row = lax.broadcasted_iota(jnp.int32, (T, T), 0)
    col = lax.broadcasted_iota(jnp.int32, (T, T), 1)
    strict = row > col
    later = jnp.where(strict, 1.0, 0.0).astype(BF16)

    qs = []
    for c in range(hp):
        qc = q_ref[0, :, c * LANES:(c + 1) * LANES]
        qs.append(jnp.where(first, qc, jnp.zeros_like(qc)))
        qs.append(jnp.where(first, jnp.zeros_like(qc), qc))

    def step(j, diagonal):
        start = pl.multiple_of(j * T, T)
        heads = range(2 * hp)
        kbs = [k_ref[0, pl.ds(start, T), c * LANES:(c + 1) * LANES] for c in range(hp)]
        vbs = [v_ref[0, pl.ds(start, T), c * LANES:(c + 1) * LANES] for c in range(hp)]
        if diagonal:
            runs = [jnp.zeros((T, 1), F32) for _ in heads]
        else:
            runs = [run_ref[h][:, 0:1] for h in heads]
        zs = [_dot_nt(qs[h], kbs[h // 2]) for h in heads]
        log_1ms = []
        for z in zs:
            neg_z = -z
            soft = jnp.log2(1.0 + jnp.exp2(jnp.minimum(z, neg_z)))
            log_1m = jnp.minimum(neg_z, 0.0) - soft
            log_1ms.append(jnp.where(strict, log_1m, 0.0) if diagonal else log_1m)
        csums = [_dot(l.astype(BF16), later) for l in log_1ms]
        pvs = []
        for h in heads:
            a = jnp.exp2(log_1ms[h] + zs[h] + (runs[h] + csums[h]))
            if diagonal:
                a = jnp.where(strict, a, 0.0)
            pvs.append(_dot(a.astype(BF16), vbs[h // 2]))
        worst = None
        for h in heads:
            run = runs[h] + jnp.sum(log_1ms[h], axis=1, keepdims=True)
            run_ref[h] = jnp.broadcast_to(run, (T, LANES))
            worst = run if worst is None else jnp.maximum(worst, run)
        for c in range(hp):
            cols = slice(c * LANES, (c + 1) * LANES)
            pv = jnp.where(first, pvs[2 * c], pvs[2 * c + 1])
            o_ref[0, :, cols] = pv if diagonal else o_ref[0, :, cols] + pv
        return (jnp.max(worst) >= SB_EXIT).astype(jnp.int32)

    go = step(qi, True)

    def cond(carry):
        j, go = carry
        return jnp.logical_and(j >= 0, go > 0)

    def body(carry):
        j, _ = carry
        return j - 1, step(j, False)

    lax.while_loop(cond, body, (qi - 1, go))


def _sb_attention(qkv, d, hp):
    b, s, _ = qkv.shape
    T = SB_TILE
    W = LANES * hp
    nblk = d // W
    kern = functools.partial(_sb_kernel, hp=hp)
    return pl.pallas_call(
        kern,
        grid=(b, nblk, s // T),
        in_specs=[pl.BlockSpec((1, T, W), lambda i, j, t: (i, t, j)),
                  pl.BlockSpec((1, s, W), lambda i, j, t: (i, 0, nblk + j)),
                  pl.BlockSpec((1, s, W), lambda i, j, t: (i, 0, 2 * nblk + j))],
        out_specs=pl.BlockSpec((1, T, W), lambda i, j, t: (i, t, j)),
        out_shape=jax.ShapeDtypeStruct((b, s, d), F32),
        scratch_shapes=[pltpu.VMEM((2 * hp, T, LANES), F32)],
        compiler_params=_params("parallel", "parallel", "arbitrary"),
        name="sb_attention",
    )(qkv, qkv, qkv)


def _sb_layer(x, gain, w_qkv, q_gain, k_gain, w_out):
    b, s, d = x.shape
    x2 = x.reshape(b * s, d)
    q_scale = math.log2(math.e) / math.sqrt(SB_HEAD_DIM)
    head_gain = jnp.concatenate([jnp.tile(q_gain * q_scale, SB_HEADS),
                                 jnp.tile(k_gain, SB_HEADS)]).reshape(1, 2 * d)
    qkv = _qkv_proj(x2, gain, w_qkv.astype(BF16), head_gain, 2 * d, TOKEN_TILE, COL_TILE)
    o = _sb_attention(qkv.reshape(b, s, 3 * d), d, SB_HEAD_PAIRS)
    return _matmul_residual(o.reshape(b * s, d), w_out.astype(BF16), x2, TOKEN_TILE).reshape(b, s, d)


def kernel(x, mix_norm, pool_in, pool_group, pool_scale, ssd_in, ssd_conv_w, ssd_conv_b, ssd_dt_bias,
           ssd_a_log, ssd_d, ssd_out_norm, ssd_out, sb_qkv, sb_q_norm, sb_k_norm, sb_out, ffn_norm,
           ffn_gate, ffn_up, ffn_down):
    b, s, d = x.shape
    depth = mix_norm.shape[0]
    for i in range(depth):
        kind, j = i % 3, i // 3
        if kind == 0:
            x = _pool_layer(x, mix_norm[i], pool_in[j].astype(BF16), pool_group[j].astype(BF16),
                            pool_scale[j], TOKEN_TILE)
        elif kind == 1:
            x = _ssd_layer(x, mix_norm[i], ssd_in[j], ssd_conv_w[j], ssd_conv_b[j], ssd_dt_bias[j],
                           ssd_a_log[j], ssd_d[j], ssd_out_norm[j], ssd_out[j])
        else:
            x = _sb_layer(x, mix_norm[i], sb_qkv[j], sb_q_norm[j], sb_k_norm[j], sb_out[j])
        x = _ffn(x.reshape(b * s, d), ffn_norm[i], ffn_gate[i].astype(BF16), ffn_up[i].astype(BF16),
                 ffn_down[i].astype(BF16), FFN_TOKEN_TILE, FFN_HIDDEN_TILE).reshape(b, s, d)
    return x
```

```python
import functools
import math

import jax
import jax.numpy as jnp
from jax import lax
from jax.experimental import pallas as pl
from jax.experimental.pallas import tpu as pltpu

F32 = jnp.float32
BF16 = jnp.bfloat16

NORM_EPS = 1e-6
POOL_WINDOWS = (2, 4, 8, 16)
POOL_HALO = 16
SSD_HEAD_DIM = 64
SSD_GROUPS = 8
SSD_HEADS_PER_GROUP = 4
SSD_STATE = 128
SSD_CONV = 4
SSD_CHUNK = 256
CONV_HALO = 8
SB_HEADS = 16
SB_HEAD_DIM = 64
SB_TILE = 256
SB_HEAD_PAIRS = 4
SB_EXIT = -152.0
LANES = 128
NEG_BIG = -1e30
TOKEN_TILE = 512
FFN_TOKEN_TILE = 1024
COL_TILE = 512
SSD_COL_TILE = 256
FFN_HIDDEN_TILE = 256

VMEM_LIMIT = 56 * 1024 * 1024


def _params(*sem):
    return pltpu.CompilerParams(dimension_semantics=sem, vmem_limit_bytes=VMEM_LIMIT)


def _resident(shape):
    return pl.BlockSpec(shape, lambda *_: (0,) * len(shape), pipeline_mode=pl.Buffered(1))


def _rms_norm(x, gain):
    return x * lax.rsqrt(jnp.mean(x * x, axis=-1, keepdims=True) + NORM_EPS) * gain


def _silu(x):
    return x / (1.0 + jnp.exp(-x))


def _softplus(x):
    return jnp.maximum(x, 0.0) + jnp.log(1.0 + jnp.exp(-jnp.abs(x)))


def _dot(a, b):
    return jnp.dot(a, b, preferred_element_type=F32)


def _dot_nt(a, b):
    return lax.dot_general(a, b, (((1,), (1,)), ((), ())), preferred_element_type=F32)


def _dot_tn(a, b):
    return lax.dot_general(a, b, (((0,), (0,)), ((), ())), preferred_element_type=F32)


def _split3(a):
    hi = a.astype(BF16)
    r = a - hi.astype(F32)
    mid = r.astype(BF16)
    lo = (r - mid.astype(F32)).astype(BF16)
    return hi, mid, lo


def _ffn_kernel(*refs, th, mixer_proj):
    if mixer_proj:
        a_ref, wp_ref, x_ref, g_ref, wg_ref, wu_ref, wd_ref, o_ref = refs
        x = x_ref[...] + _dot(a_ref[...], wp_ref[...])
    else:
        x_ref, g_ref, wg_ref, wu_ref, wd_ref, o_ref = refs
        x = x_ref[...]
    h = _rms_norm(x, g_ref[...]).astype(BF16)
    acc = x
    for k in range(wg_ref.shape[1] // th):
        cols = slice(k * th, (k + 1) * th)
        a = _silu(_dot(h, wg_ref[:, cols])) * _dot(h, wu_ref[:, cols])
        acc = acc + _dot(a.astype(BF16), wd_ref[cols, :])
    o_ref[...] = acc


def _ffn(x, gain, wg, wu, wd, tm, th, mixer_out=None, w_proj=None):
    t, d = x.shape
    hid = wg.shape[1]
    in_specs = [pl.BlockSpec((tm, d), lambda i: (i, 0)),
                _resident((1, d)),
                _resident((d, hid)),
                _resident((d, hid)),
                _resident((hid, d))]
    args = (x, gain.reshape(1, d), wg, wu, wd)
    if mixer_out is not None:
        k = mixer_out.shape[1]
        in_specs = [pl.BlockSpec((tm, k), lambda i: (i, 0)), _resident((k, d))] + in_specs
        args = (mixer_out, w_proj) + args
    return pl.pallas_call(
        functools.partial(_ffn_kernel, th=th, mixer_proj=mixer_out is not None),
        grid=(t // tm,),
        in_specs=in_specs,
        out_specs=pl.BlockSpec((tm, d), lambda i: (i, 0)),
        out_shape=jax.ShapeDtypeStruct((t, d), F32),
        compiler_params=_params("parallel"),
        name="ffn",
    )(*args)


def _pool_kernel(x_ref, g_ref, win_ref, wgrp_ref, sc_ref, o_ref, carry_ref, *, ts, gd):
    s = pl.program_id(1)

    @pl.when(s == 0)
    def _():
        carry_ref[...] = jnp.zeros_like(carry_ref)

    x = x_ref[0]
    h = _rms_norm(x, g_ref[...]).astype(BF16)
    u = _dot(h, win_ref[...])
    ext = jnp.concatenate([carry_ref[...], u], axis=0)
    carry_ref[...] = u[ts - POOL_HALO:, :]

    pos = s * ts + lax.broadcasted_iota(jnp.int32, (ts, 1), 0)
    for g, w in enumerate(POOL_WINDOWS):
        cols = slice(g * gd, (g + 1) * gd)
        acc = ext[:, cols]
        span = 1
        while span < w:
            acc = acc + pltpu.roll(acc, span, axis=0)
            span *= 2
        inv_count = 1.0 / jnp.minimum(pos + 1, w).astype(F32)
        p = acc[POOL_HALO:, :] * inv_count - u[:, cols]
        y = _dot(p.astype(BF16), wgrp_ref[g])
        o_ref[0, :, cols] = x[:, cols] + y * sc_ref[:, cols]


def _pool_layer(x, gain, w_in, w_group, scale, ts):
    b, s, d = x.shape
    ng, gd, _ = w_group.shape
    kern = functools.partial(_pool_kernel, ts=ts, gd=gd)
    return pl.pallas_call(
        kern,
        grid=(b, s // ts),
        in_specs=[pl.BlockSpec((1, ts, d), lambda i, j: (i, j, 0)),
                  _resident((1, d)),
                  _resident((d, d)),
                  _resident((ng, gd, gd)),
                  _resident((1, d))],
        out_specs=pl.BlockSpec((1, ts, d), lambda i, j: (i, j, 0)),
        out_shape=jax.ShapeDtypeStruct((b, s, d), F32),
        scratch_shapes=[pltpu.VMEM((POOL_HALO, d), F32)],
        compiler_params=_params("parallel", "arbitrary"),
        name="pool_mixer",
    )(x, gain.reshape(1, d), w_in, w_group, scale.reshape(1, d))


def _ssd_in_kernel(x_ref, g_ref, wz_ref, wxbc_ref, wdt_ref, wdtt_ref, cw_ref, cb_ref, dtb_r_ref,
                   dtb_c_ref, z_ref, xbc_ref, dtc_ref, dtr_ref, halo_ref, *, tiles_per_seq, tn):
    assert SSD_CONV == 4
    i = pl.program_id(0)
    tm = x_ref.shape[0]
    h = _rms_norm(x_ref[...], g_ref[...]).astype(BF16)

    dtc_ref[...] = _softplus(_dot(h, wdt_ref[...]) + dtb_r_ref[...])
    dtr_ref[...] = _softplus(_dot_nt(wdtt_ref[...], h) + dtb_c_ref[...])

    @pl.when(i % tiles_per_seq == 0)
    def _():
        halo_ref[...] = jnp.zeros_like(halo_ref)

    def z_chunk(c):
        cols = slice(c * tn, (c + 1) * tn)
        z_ref[:, cols] = _dot(h, wz_ref[:, cols]).astype(z_ref.dtype)

    def xbc_chunk(c):
        cols = slice(c * tn, (c + 1) * tn)
        raw = _dot(h, wxbc_ref[:, cols])
        ext = jnp.concatenate([halo_ref[:, cols], raw], axis=0)
        halo_ref[:, cols] = raw[tm - CONV_HALO:, :]
        prev = pltpu.roll(ext, 1, axis=0)
        p = ext * cw_ref[3:4, cols] + prev * cw_ref[2:3, cols]
        q = ext * cw_ref[1:2, cols] + prev * cw_ref[0:1, cols]
        acc = (p + pltpu.roll(q, 2, axis=0))[CONV_HALO:, :]
        xbc_ref[:, cols] = _silu(acc + cb_ref[:, cols]).astype(xbc_ref.dtype)

    nz, nxbc = wz_ref.shape[1] // tn, wxbc_ref.shape[1] // tn
    steps = max(nz, nxbc)
    for c in range(steps):
        for k in range(c * nxbc // steps, (c + 1) * nxbc // steps):
            xbc_chunk(k)
        for k in range(c * nz // steps, (c + 1) * nz // steps):
            z_chunk(k)


def _ssd_in_proj(x, gain, w_z, w_xbc, w_dt, conv_w, conv_b, dt_bias, tiles_per_seq, tm, tn):
    t, d = x.shape
    nz, nxbc, nh = w_z.shape[1], w_xbc.shape[1], w_dt.shape[1]
    kern = functools.partial(_ssd_in_kernel, tiles_per_seq=tiles_per_seq, tn=tn)
    return pl.pallas_call(
        kern,
        grid=(t // tm,),
        in_specs=[pl.BlockSpec((tm, d), lambda i: (i, 0)),
                  _resident((1, d)),
                  _resident((d, nz)),
                  _resident((d, nxbc)),
                  _resident((d, nh)),
                  _resident((nh, d)),
                  _resident((SSD_CONV, nxbc)),
                  _resident((1, nxbc)),
                  _resident((1, nh)),
                  _resident((nh, 1))],
        out_specs=[pl.BlockSpec((tm, nz), lambda i: (i, 0)),
                   pl.BlockSpec((tm, nxbc), lambda i: (i, 0)),
                   pl.BlockSpec((tm, nh), lambda i: (i, 0)),
                   pl.BlockSpec((nh, tm), lambda i: (0, i))],
        out_shape=[jax.ShapeDtypeStruct((t, nz), BF16),
                   jax.ShapeDtypeStruct((t, nxbc), BF16),
                   jax.ShapeDtypeStruct((t, nh), F32),
                   jax.ShapeDtypeStruct((nh, t), F32)],
        scratch_shapes=[pltpu.VMEM((CONV_HALO, nxbc), F32)],
        compiler_params=_params("arbitrary"),
        name="ssd_in_proj",
    )(x, gain.reshape(1, d), w_z, w_xbc, w_dt, w_dt.T, conv_w, conv_b.reshape(1, nxbc),
      dt_bias.reshape(1, nh), dt_bias.reshape(nh, 1))


def _ssd_kernel(z_ref, xbc_ref, dtc_ref, dtr_ref, res_ref, alog_r_ref, alog_c_ref, dsk_ref,
                onorm_ref, wout_ref, o_ref, state_ref, gated_ref):
    L, P, HG, G, N = SSD_CHUNK, SSD_HEAD_DIM, SSD_HEADS_PER_GROUP, SSD_GROUPS, SSD_STATE
    GW = HG * P
    boff, coff = G * GW, G * GW + G * N

    @pl.when(pl.program_id(1) == 0)
    def _():
        state_ref[...] = jnp.zeros_like(state_ref)

    dt_c = dtc_ref[0]
    da_c = dt_c * (-jnp.exp(alog_r_ref[...]))
    da_r = dtr_ref[...] * (-jnp.exp(alog_c_ref[...]))
    row = lax.broadcasted_iota(jnp.int32, (L, L), 0)
    col = lax.broadcasted_iota(jnp.int32, (L, L), 1)
    causal = row >= col
    tri = jnp.where(causal, 1.0, 0.0).astype(BF16)
    tri_t = jnp.where(row <= col, 1.0, 0.0).astype(BF16)
    acum_c = sum(_dot(tri, piece) for piece in _split3(da_c))
    acum_r = sum(_dot(piece, tri_t) for piece in _split3(da_r))
    a_last = acum_c[L - 1:L, :]

    nh = dt_c.shape[1]
    head_of_col = lax.shift_right_logical(lax.broadcasted_iota(jnp.int32, (nh, nh * P), 1),
                                          int(math.log2(P)))
    expand = jnp.where(head_of_col == lax.broadcasted_iota(jnp.int32, (nh, nh * P), 0),
                       1.0, 0.0).astype(BF16)
    dt_w = _dot(dt_c.astype(BF16), expand)
    tail_w = _dot((dt_c * jnp.exp(a_last - acum_c)).astype(BF16), expand)
    grow_hi, grow_mid, _ = _split3(jnp.exp(acum_c))
    grow_w = _dot(grow_hi, expand) + _dot(grow_mid, expand)
    head_of_lane = lax.shift_right_logical(lax.broadcasted_iota(jnp.int32, (1, GW), 1),
                                           int(math.log2(P)))

    groups = range(G)
    gcols = [slice(g * GW, (g + 1) * GW) for g in groups]
    xs = [xbc_ref[0, :, gcols[g]].astype(F32) for g in groups]
    bm = [xbc_ref[0, :, boff + g * N:boff + (g + 1) * N] for g in groups]
    cm = [xbc_ref[0, :, coff + g * N:coff + (g + 1) * N] for g in groups]
    st = [state_ref[g] for g in groups]
    cb = [_dot_nt(cm[g], bm[g]) for g in groups]
    carried = [_dot(cm[g], st[g].astype(BF16)) for g in groups]
    xdt = [(xs[g] * dt_w[:, gcols[g]]).astype(BF16) for g in groups]

    for g in groups:
        xw = (xs[g] * tail_w[:, gcols[g]]).astype(BF16)
        state_ref[g] = st[g] * grow_w[L - 1:L, gcols[g]] + _dot_tn(bm[g], xw)

    for g in groups:
        y = carried[g] * grow_w[:, gcols[g]] + dsk_ref[:, gcols[g]] * xs[g]
        for hh in range(HG):
            hd = g * HG + hh
            decay = jnp.exp(jnp.where(causal, acum_c[:, hd:hd + 1] - acum_r[hd:hd + 1, :], NEG_BIG))
            y = y + _dot((cb[g] * decay).astype(BF16),
                         jnp.where(head_of_lane == hh, xdt[g], jnp.zeros_like(xdt[g])))
        gated = y * _silu(z_ref[0, :, gcols[g]].astype(F32))
        gated = gated * lax.rsqrt(jnp.mean(gated * gated, axis=-1, keepdims=True) + NORM_EPS)
        gated_ref[:, gcols[g]] = (gated * onorm_ref[:, gcols[g]]).astype(BF16)

    o_ref[0] = res_ref[0] + _dot(gated_ref[...], wout_ref[...])


def _ssd_layer(x, gain, w_in, conv_w, conv_b, dt_bias, a_log, d_skip, out_norm, w_out):
    b, s, d = x.shape
    L, G, N = SSD_CHUNK, SSD_GROUPS, SSD_STATE
    GW = SSD_HEADS_PER_GROUP * SSD_HEAD_DIM
    d_inner = G * GW
    nxbc = d_inner + 2 * G * N
    nh = G * SSD_HEADS_PER_GROUP
    nchunks = s // L
    w_z = w_in[:, :d_inner].astype(BF16)
    w_xbc = w_in[:, d_inner:d_inner + nxbc].astype(BF16)
    w_dt = w_in[:, d_inner + nxbc:].astype(BF16)
    z, xbc, dt_c, dt_r = _ssd_in_proj(x.reshape(b * s, d), gain, w_z, w_xbc, w_dt, conv_w, conv_b,
                                      dt_bias, s // TOKEN_TILE, TOKEN_TILE, SSD_COL_TILE)
    return pl.pallas_call(
        _ssd_kernel,
        grid=(b, nchunks),
        in_specs=[pl.BlockSpec((1, L, d_inner), lambda i, c: (i, c, 0)),
                  pl.BlockSpec((1, L, nxbc), lambda i, c: (i, c, 0)),
                  pl.BlockSpec((1, L, nh), lambda i, c: (i, c, 0)),
                  pl.BlockSpec((nh, L), lambda i, c: (0, i * nchunks + c)),
                  pl.BlockSpec((1, L, d), lambda i, c: (i, c, 0)),
                  _resident((1, nh)),
                  _resident((nh, 1)),
                  _resident((1, d_inner)),
                  _resident((1, d_inner)),
                  _resident((d_inner, d))],
        out_specs=pl.BlockSpec((1, L, d), lambda i, c: (i, c, 0)),
        out_shape=jax.ShapeDtypeStruct((b, s, d), F32),
        scratch_shapes=[pltpu.VMEM((G, N, GW), F32),
                        pltpu.VMEM((L, d_inner), BF16)],
        compiler_params=_params("parallel", "arbitrary"),
        name="ssd_scan",
    )(z.reshape(b, s, d_inner), xbc.reshape(b, s, nxbc), dt_c.reshape(b, s, nh), dt_r, x,
      a_log.reshape(1, nh), a_log.reshape(nh, 1), jnp.repeat(d_skip, SSD_HEAD_DIM).reshape(1, d_inner),
      out_norm.reshape(1, d_inner), w_out.astype(BF16))


def _first_head_lanes():
    return lax.broadcasted_iota(jnp.int32, (1, LANES), 1) < SB_HEAD_DIM


def _qkv_kernel(x_ref, g_ref, w_ref, hg_ref, o_ref, *, n_norm_cols, tn):
    h = _rms_norm(x_ref[...], g_ref[...]).astype(BF16)
    first = _first_head_lanes()
    for j in range(w_ref.shape[1] // tn):
        y = _dot(h, w_ref[:, j * tn:(j + 1) * tn])
        if j * tn >= n_norm_cols:
            o_ref[:, j * tn:(j + 1) * tn] = y.astype(o_ref.dtype)
            continue
        for c in range(tn // LANES):
            cols = slice(j * tn + c * LANES, j * tn + (c + 1) * LANES)
            blk = y[:, c * LANES:(c + 1) * LANES]
            sq = blk * blk
            ms0 = jnp.sum(jnp.where(first, sq, 0.0), axis=-1, keepdims=True) * (1.0 / SB_HEAD_DIM)
            ms1 = jnp.sum(jnp.where(first, 0.0, sq), axis=-1, keepdims=True) * (1.0 / SB_HEAD_DIM)
            inv = jnp.where(first, lax.rsqrt(ms0 + NORM_EPS), lax.rsqrt(ms1 + NORM_EPS))
            o_ref[:, cols] = (blk * inv * hg_ref[:, cols]).astype(o_ref.dtype)


def _qkv_proj(x, gain, w, head_gain, n_norm_cols, tm, tn):
    t, d = x.shape
    n = w.shape[1]
    kern = functools.partial(_qkv_kernel, n_norm_cols=n_norm_cols, tn=tn)
    return pl.pallas_call(
        kern,
        grid=(t // tm,),
        in_specs=[pl.BlockSpec((tm, d), lambda i: (i, 0)),
                  _resident((1, d)),
                  _resident((d, n)),
                  _resident((1, n_norm_cols))],
        out_specs=pl.BlockSpec((tm, n), lambda i: (i, 0)),
        out_shape=jax.ShapeDtypeStruct((t, n), BF16),
        compiler_params=_params("parallel"),
        name="sb_qkv_proj",
    )(x, gain.reshape(1, d), w, head_gain)


def _sb_kernel(q_ref, k_ref, v_ref, o_ref, run_ref, acc_ref, *, hp):
    qi = pl.program_id(2)
    T = SB_TILE
    first = _first_head_lanes()
    row = lax.broadcasted_iota(jnp.int32, (T, T), 0)
    col = lax.broadcasted_iota(jnp.int32, (T, T), 1)
    strict = row > col
    later = jnp.where(strict, 1.0, 0.0).astype(BF16)

    qs = []
    for c in range(hp):
        qc = q_ref[0, :, c * LANES:(c + 1) * LANES]
        qs.append(jnp.where(first, qc, jnp.zeros_like(qc)))
        qs.append(jnp.where(first, jnp.zeros_like(qc), qc))

    def step(j, diagonal):
        start = pl.multiple_of(j * T, T)
        heads = range(2 * hp)
        kbs = [k_ref[0, pl.ds(start, T), c * LANES:(c + 1) * LANES] for c in range(hp)]
        vbs = [v_ref[0, pl.ds(start, T), c * LANES:(c + 1) * LANES] for c in range(hp)]
        if diagonal:
            runs = [jnp.zeros((T, 1), F32) for _ in heads]
        else:
            runs = [run_ref[h][:, 0:1] for h in heads]
        zs = [_dot_nt(qs[h], kbs[h // 2]) for h in heads]
        log_1ms = []
        for z in zs:
            neg_z = -z
            soft = jnp.log2(1.0 + jnp.exp2(jnp.minimum(z, neg_z)))
            log_1m = jnp.minimum(neg_z, 0.0) - soft
            log_1ms.append(jnp.where(strict, log_1m, 0.0) if diagonal else log_1m)
        csums = [_dot(l.astype(BF16), later) for l in log_1ms]
        pvs = []
        for h in heads:
            a = jnp.exp2(log_1ms[h] + zs[h] + (runs[h] + csums[h]))
            if diagonal:
                a = jnp.where(strict, a, 0.0)
            pvs.append(_dot(a.astype(BF16), vbs[h // 2]))
        worst = None
        for h in heads:
            run = runs[h] + jnp.sum(log_1ms[h], axis=1, keepdims=True)
            run_ref[h] = jnp.broadcast_to(run, (T, LANES))
            worst = run if worst is None else jnp.maximum(worst, run)
        for c in range(hp):
            cols = slice(c * LANES, (c + 1) * LANES)
            pv = jnp.where(first, pvs[2 * c], pvs[2 * c + 1])
            acc_ref[:, cols] = pv if diagonal else acc_ref[:, cols] + pv
        return (jnp.max(worst) >= SB_EXIT).astype(jnp.int32)

    go = step(qi, True)

    def cond(carry):
        j, go = carry
        return jnp.logical_and(j >= 0, go > 0)

    def body(carry):
        j, _ = carry
        return j - 1, step(j, False)

    lax.while_loop(cond, body, (qi - 1, go))
    o_ref[0] = acc_ref[...].astype(o_ref.dtype)


def _sb_attention(qkv, d, hp):
    b, s, _ = qkv.shape
    T = SB_TILE
    W = LANES * hp
    nblk = d // W
    kern = functools.partial(_sb_kernel, hp=hp)
    return pl.pallas_call(
        kern,
        grid=(b, nblk, s // T),
        in_specs=[pl.BlockSpec((1, T, W), lambda i, j, t: (i, t, j)),
                  pl.BlockSpec((1, s, W), lambda i, j, t: (i, 0, nblk + j)),
                  pl.BlockSpec((1, s, W), lambda i, j, t: (i, 0, 2 * nblk + j))],
        out_specs=pl.BlockSpec((1, T, W), lambda i, j, t: (i, t, j)),
        out_shape=jax.ShapeDtypeStruct((b, s, d), BF16),
        scratch_shapes=[pltpu.VMEM((2 * hp, T, LANES), F32), pltpu.VMEM((T, W), F32)],
        compiler_params=_params("parallel", "parallel", "arbitrary"),
        name="sb_attention",
    )(qkv, qkv, qkv)


def _sb_heads(x, gain, w_qkv, q_gain, k_gain):
    b, s, d = x.shape
    x2 = x.reshape(b * s, d)
    q_scale = math.log2(math.e) / math.sqrt(SB_HEAD_DIM)
    head_gain = jnp.concatenate([jnp.tile(q_gain * q_scale, SB_HEADS),
                                 jnp.tile(k_gain, SB_HEADS)]).reshape(1, 2 * d)
    qkv = _qkv_proj(x2, gain, w_qkv.astype(BF16), head_gain, 2 * d, TOKEN_TILE, COL_TILE)
    return _sb_attention(qkv.reshape(b, s, 3 * d), d, SB_HEAD_PAIRS).reshape(b * s, d)


def kernel(x, mix_norm, pool_in, pool_group, pool_scale, ssd_in, ssd_conv_w, ssd_conv_b, ssd_dt_bias,
           ssd_a_log, ssd_d, ssd_out_norm, ssd_out, sb_qkv, sb_q_norm, sb_k_norm, sb_out, ffn_norm,
           ffn_gate, ffn_up, ffn_down):
    b, s, d = x.shape
    depth = mix_norm.shape[0]
    for i in range(depth):
        kind, j = i % 3, i // 3
        mixer_out = w_proj = None
        if kind == 0:
            x = _pool_layer(x, mix_norm[i], pool_in[j].astype(BF16), pool_group[j].astype(BF16),
                            pool_scale[j], TOKEN_TILE)
        elif kind == 1:
            x = _ssd_layer(x, mix_norm[i], ssd_in[j], ssd_conv_w[j], ssd_conv_b[j], ssd_dt_bias[j],
                           ssd_a_log[j], ssd_d[j], ssd_out_norm[j], ssd_out[j])
        else:
            mixer_out = _sb_heads(x, mix_norm[i], sb_qkv[j], sb_q_norm[j], sb_k_norm[j])
            w_proj = sb_out[j].astype(BF16)
        x = _ffn(x.reshape(b * s, d), ffn_norm[i], ffn_gate[i].astype(BF16), ffn_up[i].astype(BF16),
                 ffn_down[i].astype(BF16), FFN_TOKEN_TILE, FFN_HIDDEN_TILE,
                 mixer_out, w_proj).reshape(b, s, d)
    return x
```

```python
import functools
import math

import jax
import jax.numpy as jnp
from jax import lax
from jax.experimental import pallas as pl
from jax.experimental.pallas import tpu as pltpu

F32 = jnp.float32
BF16 = jnp.bfloat16

NORM_EPS = 1e-6
POOL_WINDOWS = (2, 4, 8, 16)
POOL_HALO = 16
SSD_HEAD_DIM = 64
SSD_GROUPS = 8
SSD_HEADS_PER_GROUP = 4
SSD_STATE = 128
SSD_CONV = 4
SSD_CHUNK = 256
CONV_HALO = 8
SB_HEADS = 16
SB_HEAD_DIM = 64
SB_TILE = 256
SB_HEAD_PAIRS = 4
SB_EXIT = -152.0
LANES = 128
NEG_BIG = -1e30
TOKEN_TILE = 512
FFN_TOKEN_TILE = 1024
COL_TILE = 512
SSD_COL_TILE = 256
FFN_HIDDEN_TILE = 256

V7X_VMEM_BYTES = 64 * 1024 * 1024
VMEM_LIMIT = V7X_VMEM_BYTES * 7 // 8


def _params(*sem):
    return pltpu.CompilerParams(dimension_semantics=sem, vmem_limit_bytes=VMEM_LIMIT)


def _resident(shape):
    return pl.BlockSpec(shape, lambda *_: (0,) * len(shape), pipeline_mode=pl.Buffered(1))


def _rms_norm(x, gain):
    return x * lax.rsqrt(jnp.mean(x * x, axis=-1, keepdims=True) + NORM_EPS) * gain


def _silu(x):
    return x / (1.0 + jnp.exp(-x))


def _softplus(x):
    return jnp.maximum(x, 0.0) + jnp.log(1.0 + jnp.exp(-jnp.abs(x)))


def _dot(a, b):
    return jnp.dot(a, b, preferred_element_type=F32)


def _dot_nt(a, b):
    return lax.dot_general(a, b, (((1,), (1,)), ((), ())), preferred_element_type=F32)


def _dot_tn(a, b):
    return lax.dot_general(a, b, (((0,), (0,)), ((), ())), preferred_element_type=F32)


def _split3(a):
    hi = a.astype(BF16)
    r = a - hi.astype(F32)
    mid = r.astype(BF16)
    lo = (r - mid.astype(F32)).astype(BF16)
    return hi, mid, lo


def _ffn_kernel(*refs, th, mixer_proj):
    if mixer_proj:
        a_ref, wp_ref, x_ref, g_ref, wg_ref, wu_ref, wd_ref, o_ref = refs
        x = x_ref[...] + _dot(a_ref[...], wp_ref[...])
    else:
        x_ref, g_ref, wg_ref, wu_ref, wd_ref, o_ref = refs
        x = x_ref[...]
    h = _rms_norm(x, g_ref[...]).astype(BF16)
    acc = x
    for k in range(wg_ref.shape[1] // th):
        cols = slice(k * th, (k + 1) * th)
        a = _silu(_dot(h, wg_ref[:, cols])) * _dot(h, wu_ref[:, cols])
        acc = acc + _dot(a.astype(BF16), wd_ref[cols, :])
    o_ref[...] = acc


def _ffn(x, gain, wg, wu, wd, tm, th, mixer_out=None, w_proj=None):
    t, d = x.shape
    hid = wg.shape[1]
    in_specs = [pl.BlockSpec((tm, d), lambda i: (i, 0)),
                _resident((1, d)),
                _resident((d, hid)),
                _resident((d, hid)),
                _resident((hid, d))]
    args = (x, gain.reshape(1, d), wg, wu, wd)
    if mixer_out is not None:
        k = mixer_out.shape[1]
        in_specs = [pl.BlockSpec((tm, k), lambda i: (i, 0)), _resident((k, d))] + in_specs
        args = (mixer_out, w_proj) + args
    return pl.pallas_call(
        functools.partial(_ffn_kernel, th=th, mixer_proj=mixer_out is not None),
        grid=(t // tm,),
        in_specs=in_specs,
        out_specs=pl.BlockSpec((tm, d), lambda i: (i, 0)),
        out_shape=jax.ShapeDtypeStruct((t, d), F32),
        compiler_params=_params("parallel"),
        name="ffn",
    )(*args)


def _pool_kernel(x_ref, g_ref, win_ref, wgrp_ref, sc_ref, o_ref, carry_ref, *, ts, gd):
    s = pl.program_id(1)

    @pl.when(s == 0)
    def _():
        carry_ref[...] = jnp.zeros_like(carry_ref)

    x = x_ref[0]
    h = _rms_norm(x, g_ref[...]).astype(BF16)
    u = _dot(h, win_ref[...])
    ext = jnp.concatenate([carry_ref[...], u], axis=0)
    carry_ref[...] = u[ts - POOL_HALO:, :]

    pos = s * ts + lax.broadcasted_iota(jnp.int32, (ts, 1), 0)
    for g, w in enumerate(POOL_WINDOWS):
        cols = slice(g * gd, (g + 1) * gd)
        acc = ext[:, cols]
        span = 1
        while span < w:
            acc = acc + pltpu.roll(acc, span, axis=0)
            span *= 2
        inv_count = 1.0 / jnp.minimum(pos + 1, w).astype(F32)
        p = acc[POOL_HALO:, :] * inv_count - u[:, cols]
        y = _dot(p.astype(BF16), wgrp_ref[g])
        o_ref[0, :, cols] = x[:, cols] + y * sc_ref[:, cols]


def _pool_layer(x, gain, w_in, w_group, scale, ts):
    b, s, d = x.shape
    ng, gd, _ = w_group.shape
    kern = functools.partial(_pool_kernel, ts=ts, gd=gd)
    return pl.pallas_call(
        kern,
        grid=(b, s // ts),
        in_specs=[pl.BlockSpec((1, ts, d), lambda i, j: (i, j, 0)),
                  _resident((1, d)),
                  _resident((d, d)),
                  _resident((ng, gd, gd)),
                  _resident((1, d))],
        out_specs=pl.BlockSpec((1, ts, d), lambda i, j: (i, j, 0)),
        out_shape=jax.ShapeDtypeStruct((b, s, d), F32),
        scratch_shapes=[pltpu.VMEM((POOL_HALO, d), F32)],
        compiler_params=_params("parallel", "arbitrary"),
        name="pool_mixer",
    )(x, gain.reshape(1, d), w_in, w_group, scale.reshape(1, d))


def _ssd_in_kernel(x_ref, g_ref, wz_ref, wxbc_ref, wdt_ref, wdtt_ref, cw_ref, cb_ref, dtb_r_ref,
                   dtb_c_ref, z_ref, xbc_ref, dtc_ref, dtr_ref, halo_ref, *, tiles_per_seq, tn):
    assert SSD_CONV == 4
    i = pl.program_id(0)
    tm = x_ref.shape[0]
    h = _rms_norm(x_ref[...], g_ref[...]).astype(BF16)

    dtc_ref[...] = _softplus(_dot(h, wdt_ref[...]) + dtb_r_ref[...])
    dtr_ref[...] = _softplus(_dot_nt(wdtt_ref[...], h) + dtb_c_ref[...])

    @pl.when(i % tiles_per_seq == 0)
    def _():
        halo_ref[...] = jnp.zeros_like(halo_ref)

    def z_chunk(c):
        cols = slice(c * tn, (c + 1) * tn)
        z_ref[:, cols] = _dot(h, wz_ref[:, cols]).astype(z_ref.dtype)

    def xbc_chunk(c):
        cols = slice(c * tn, (c + 1) * tn)
        raw = _dot(h, wxbc_ref[:, cols])
        ext = jnp.concatenate([halo_ref[:, cols], raw], axis=0)
        halo_ref[:, cols] = raw[tm - CONV_HALO:, :]
        prev = pltpu.roll(ext, 1, axis=0)
        p = ext * cw_ref[3:4, cols] + prev * cw_ref[2:3, cols]
        q = ext * cw_ref[1:2, cols] + prev * cw_ref[0:1, cols]
        acc = (p + pltpu.roll(q, 2, axis=0))[CONV_HALO:, :]
        xbc_ref[:, cols] = _silu(acc + cb_ref[:, cols]).astype(xbc_ref.dtype)

    nz, nxbc = wz_ref.shape[1] // tn, wxbc_ref.shape[1] // tn
    steps = max(nz, nxbc)
    for c in range(steps):
        for k in range(c * nxbc // steps, (c + 1) * nxbc // steps):
            xbc_chunk(k)
        for k in range(c * nz // steps, (c + 1) * nz // steps):
            z_chunk(k)


def _ssd_in_proj(x, gain, w_z, w_xbc, w_dt, conv_w, conv_b, dt_bias, tiles_per_seq, tm, tn):
    t, d = x.shape
    nz, nxbc, nh = w_z.shape[1], w_xbc.shape[1], w_dt.shape[1]
    kern = functools.partial(_ssd_in_kernel, tiles_per_seq=tiles_per_seq, tn=tn)
    return pl.pallas_call(
        kern,
        grid=(t // tm,),
        in_specs=[pl.BlockSpec((tm, d), lambda i: (i, 0)),
                  _resident((1, d)),
                  _resident((d, nz)),
                  _resident((d, nxbc)),
                  _resident((d, nh)),
                  _resident((nh, d)),
                  _resident((SSD_CONV, nxbc)),
                  _resident((1, nxbc)),
                  _resident((1, nh)),
                  _resident((nh, 1))],
        out_specs=[pl.BlockSpec((tm, nz), lambda i: (i, 0)),
                   pl.BlockSpec((tm, nxbc), lambda i: (i, 0)),
                   pl.BlockSpec((tm, nh), lambda i: (i, 0)),
                   pl.BlockSpec((nh, tm), lambda i: (0, i))],
        out_shape=[jax.ShapeDtypeStruct((t, nz), BF16),
                   jax.ShapeDtypeStruct((t, nxbc), BF16),
                   jax.ShapeDtypeStruct((t, nh), F32),
                   jax.ShapeDtypeStruct((nh, t), F32)],
        scratch_shapes=[pltpu.VMEM((CONV_HALO, nxbc), F32)],
        compiler_params=_params("arbitrary"),
        name="ssd_in_proj",
    )(x, gain.reshape(1, d), w_z, w_xbc, w_dt, w_dt.T, conv_w, conv_b.reshape(1, nxbc),
      dt_bias.reshape(1, nh), dt_bias.reshape(nh, 1))


def _ssd_kernel(z_ref, xbc_ref, dtc_ref, dtr_ref, res_ref, alog_r_ref, alog_c_ref, dsk_ref,
                onorm_ref, wout_ref, o_ref, state_ref, gated_ref):
    L, P, HG, G, N = SSD_CHUNK, SSD_HEAD_DIM, SSD_HEADS_PER_GROUP, SSD_GROUPS, SSD_STATE
    GW = HG * P
    boff, coff = G * GW, G * GW + G * N

    @pl.when(pl.program_id(1) == 0)
    def _():
        state_ref[...] = jnp.zeros_like(state_ref)

    dt_c = dtc_ref[0]
    da_c = dt_c * (-jnp.exp(alog_r_ref[...]))
    da_r = dtr_ref[...] * (-jnp.exp(alog_c_ref[...]))
    row = lax.broadcasted_iota(jnp.int32, (L, L), 0)
    col = lax.broadcasted_iota(jnp.int32, (L, L), 1)
    causal = row >= col
    tri = jnp.where(causal, 1.0, 0.0).astype(BF16)
    tri_t = jnp.where(row <= col, 1.0, 0.0).astype(BF16)
    acum_c = sum(_dot(tri, piece) for piece in _split3(da_c))
    acum_r = sum(_dot(piece, tri_t) for piece in _split3(da_r))
    a_last = acum_c[L - 1:L, :]

    nh = dt_c.shape[1]
    head_of_col = lax.shift_right_logical(lax.broadcasted_iota(jnp.int32, (nh, nh * P), 1),
                                          int(math.log2(P)))
    expand = jnp.where(head_of_col == lax.broadcasted_iota(jnp.int32, (nh, nh * P), 0),
                       1.0, 0.0).astype(BF16)
    dt_w = _dot(dt_c.astype(BF16), expand)
    tail_w = _dot((dt_c * jnp.exp(a_last - acum_c)).astype(BF16), expand)
    grow_hi, grow_mid, _ = _split3(jnp.exp(acum_c))
    grow_w = _dot(grow_hi, expand) + _dot(grow_mid, expand)
    head_of_lane = lax.shift_right_logical(lax.broadcasted_iota(jnp.int32, (1, GW), 1),
                                           int(math.log2(P)))

    groups = range(G)
    gcols = [slice(g * GW, (g + 1) * GW) for g in groups]
    xs = [xbc_ref[0, :, gcols[g]].astype(F32) for g in groups]
    bm = [xbc_ref[0, :, boff + g * N:boff + (g + 1) * N] for g in groups]
    cm = [xbc_ref[0, :, coff + g * N:coff + (g + 1) * N] for g in groups]
    st = [state_ref[g] for g in groups]
    cb = [_dot_nt(cm[g], bm[g]) for g in groups]
    carried = [_dot(cm[g], st[g].astype(BF16)) for g in groups]
    xdt = [(xs[g] * dt_w[:, gcols[g]]).astype(BF16) for g in groups]

    for g in groups:
        xw = (xs[g] * tail_w[:, gcols[g]]).astype(BF16)
        state_ref[g] = st[g] * grow_w[L - 1:L, gcols[g]] + _dot_tn(bm[g], xw)

    for g in groups:
        y = carried[g] * grow_w[:, gcols[g]] + dsk_ref[:, gcols[g]] * xs[g]
        for hh in range(HG):
            hd = g * HG + hh
            decay = jnp.exp(jnp.where(causal, acum_c[:, hd:hd + 1] - acum_r[hd:hd + 1, :], NEG_BIG))
            y = y + _dot((cb[g] * decay).astype(BF16),
                         jnp.where(head_of_lane == hh, xdt[g], jnp.zeros_like(xdt[g])))
        gated = y * _silu(z_ref[0, :, gcols[g]].astype(F32))
        gated = gated * lax.rsqrt(jnp.mean(gated * gated, axis=-1, keepdims=True) + NORM_EPS)
        gated_ref[:, gcols[g]] = (gated * onorm_ref[:, gcols[g]]).astype(BF16)

    o_ref[0] = res_ref[0] + _dot(gated_ref[...], wout_ref[...])


def _ssd_layer(x, gain, w_in, conv_w, conv_b, dt_bias, a_log, d_skip, out_norm, w_out):
    b, s, d = x.shape
    L, G, N = SSD_CHUNK, SSD_GROUPS, SSD_STATE
    GW = SSD_HEADS_PER_GROUP * SSD_HEAD_DIM
    d_inner = G * GW
    nxbc = d_inner + 2 * G * N
    nh = G * SSD_HEADS_PER_GROUP
    nchunks = s // L
    w_z = w_in[:, :d_inner].astype(BF16)
    w_xbc = w_in[:, d_inner:d_inner + nxbc].astype(BF16)
    w_dt = w_in[:, d_inner + nxbc:].astype(BF16)
    z, xbc, dt_c, dt_r = _ssd_in_proj(x.reshape(b * s, d), gain, w_z, w_xbc, w_dt, conv_w, conv_b,
                                      dt_bias, s // TOKEN_TILE, TOKEN_TILE, SSD_COL_TILE)
    return pl.pallas_call(
        _ssd_kernel,
        grid=(b, nchunks),
        in_specs=[pl.BlockSpec((1, L, d_inner), lambda i, c: (i, c, 0)),
                  pl.BlockSpec((1, L, nxbc), lambda i, c: (i, c, 0)),
                  pl.BlockSpec((1, L, nh), lambda i, c: (i, c, 0)),
                  pl.BlockSpec((nh, L), lambda i, c: (0, i * nchunks + c)),
                  pl.BlockSpec((1, L, d), lambda i, c: (i, c, 0)),
                  _resident((1, nh)),
                  _resident((nh, 1)),
                  _resident((1, d_inner)),
                  _resident((1, d_inner)),
                  _resident((d_inner, d))],
        out_specs=pl.BlockSpec((1, L, d), lambda i, c: (i, c, 0)),
        out_shape=jax.ShapeDtypeStruct((b, s, d), F32),
        scratch_shapes=[pltpu.VMEM((G, N, GW), F32),
                        pltpu.VMEM((L, d_inner), BF16)],
        compiler_params=_params("parallel", "arbitrary"),
        name="ssd_scan",
    )(z.reshape(b, s, d_inner), xbc.reshape(b, s, nxbc), dt_c.reshape(b, s, nh), dt_r, x,
      a_log.reshape(1, nh), a_log.reshape(nh, 1), jnp.repeat(d_skip, SSD_HEAD_DIM).reshape(1, d_inner),
      out_norm.reshape(1, d_inner), w_out.astype(BF16))


def _first_head_lanes():
    return lax.broadcasted_iota(jnp.int32, (1, LANES), 1) < SB_HEAD_DIM


def _qkv_kernel(x_ref, g_ref, w_ref, hg_ref, o_ref, *, n_norm_cols, tn):
    h = _rms_norm(x_ref[...], g_ref[...]).astype(BF16)
    first = _first_head_lanes()
    for j in range(w_ref.shape[1] // tn):
        y = _dot(h, w_ref[:, j * tn:(j + 1) * tn])
        if j * tn >= n_norm_cols:
            o_ref[:, j * tn:(j + 1) * tn] = y.astype(o_ref.dtype)
            continue
        for c in range(tn // LANES):
            cols = slice(j * tn + c * LANES, j * tn + (c + 1) * LANES)
            blk = y[:, c * LANES:(c + 1) * LANES]
            sq = blk * blk
            ms0 = jnp.sum(jnp.where(first, sq, 0.0), axis=-1, keepdims=True) * (1.0 / SB_HEAD_DIM)
            ms1 = jnp.sum(jnp.where(first, 0.0, sq), axis=-1, keepdims=True) * (1.0 / SB_HEAD_DIM)
            inv = jnp.where(first, lax.rsqrt(ms0 + NORM_EPS), lax.rsqrt(ms1 + NORM_EPS))
            o_ref[:, cols] = (blk * inv * hg_ref[:, cols]).astype(o_ref.dtype)


def _qkv_proj(x, gain, w, head_gain, n_norm_cols, tm, tn):
    t, d = x.shape
    n = w.shape[1]
    kern = functools.partial(_qkv_kernel, n_norm_cols=n_norm_cols, tn=tn)
    return pl.pallas_call(
        kern,
        grid=(t // tm,),
        in_specs=[pl.BlockSpec((tm, d), lambda i: (i, 0)),
                  _resident((1, d)),
                  _resident((d, n)),
                  _resident((1, n_norm_cols))],
        out_specs=pl.BlockSpec((tm, n), lambda i: (i, 0)),
        out_shape=jax.ShapeDtypeStruct((t, n), BF16),
        compiler_params=_params("parallel"),
        name="sb_qkv_proj",
    )(x, gain.reshape(1, d), w, head_gain)


def _sb_kernel(q_ref, k_ref, v_ref, o_ref, run_ref, acc_ref, *, hp):
    qi = pl.program_id(2)
    T = SB_TILE
    first = _first_head_lanes()
    row = lax.broadcasted_iota(jnp.int32, (T, T), 0)
    col = lax.broadcasted_iota(jnp.int32, (T, T), 1)
    strict = row > col
    later = jnp.where(strict, 1.0, 0.0).astype(BF16)

    qs = []
    for c in range(hp):
        qc = q_ref[0, :, c * LANES:(c + 1) * LANES]
        qs.append(jnp.where(first, qc, jnp.zeros_like(qc)))
        qs.append(jnp.where(first, jnp.zeros_like(qc), qc))

    half = T // 2
    strict_q = strict[:half, :half]

    def quadrants(t):
        return t[:half, :half], t[half:, :half], t[half:, half:]

    def assemble(ul, ll, lr):
        return jnp.concatenate([jnp.concatenate([ul, jnp.zeros_like(ul)], axis=1),
                                jnp.concatenate([ll, lr], axis=1)], axis=0)

    def log_one_minus_beta(z):
        neg_z = -z
        return jnp.minimum(neg_z, 0.0) - jnp.log2(1.0 + jnp.exp2(jnp.minimum(z, neg_z)))

    def step(j, diagonal):
        start = pl.multiple_of(j * T, T)
        heads = range(2 * hp)
        kbs = [k_ref[0, pl.ds(start, T), c * LANES:(c + 1) * LANES] for c in range(hp)]
        vbs = [v_ref[0, pl.ds(start, T), c * LANES:(c + 1) * LANES] for c in range(hp)]
        if diagonal:
            runs = [jnp.zeros((T, 1), F32) for _ in heads]
        else:
            runs = [run_ref[h][:, 0:1] for h in heads]
        zs = [_dot_nt(qs[h], kbs[h // 2]) for h in heads]
        if diagonal:
            log_1ms = [assemble(*[jnp.where(strict_q, log_one_minus_beta(zq), 0.0) if on_diag
                                  else log_one_minus_beta(zq)
                                  for zq, on_diag in zip(quadrants(z), (True, False, True))])
                       for z in zs]
        else:
            log_1ms = [log_one_minus_beta(z) for z in zs]
        csums = [_dot(l.astype(BF16), later) for l in log_1ms]
        pvs = []
        for h in heads:
            if diagonal:
                parts = zip(quadrants(log_1ms[h]), quadrants(zs[h]), quadrants(csums[h]),
                            (True, False, True))
                a = assemble(*[jnp.where(strict_q, jnp.exp2(lq + zq + cq), 0.0) if on_diag
                               else jnp.exp2(lq + zq + cq) for lq, zq, cq, on_diag in parts])
            else:
                a = jnp.exp2(log_1ms[h] + zs[h] + (runs[h] + csums[h]))
            pvs.append(_dot(a.astype(BF16), vbs[h // 2]))
        worst = None
        for h in heads:
            run = runs[h] + jnp.sum(log_1ms[h], axis=1, keepdims=True)
            run_ref[h] = jnp.broadcast_to(run, (T, LANES))
            worst = run if worst is None else jnp.maximum(worst, run)
        for c in range(hp):
            cols = slice(c * LANES, (c + 1) * LANES)
            pv = jnp.where(first, pvs[2 * c], pvs[2 * c + 1])
            acc_ref[:, cols] = pv if diagonal else acc_ref[:, cols] + pv
        return (jnp.max(worst) >= SB_EXIT).astype(jnp.int32)

    go = step(qi, True)

    def cond(carry):
        j, go = carry
        return jnp.logical_and(j >= 0, go > 0)

    def body(carry):
        j, _ = carry
        return j - 1, step(j, False)

    lax.while_loop(cond, body, (qi - 1, go))
    o_ref[0] = acc_ref[...].astype(o_ref.dtype)


def _sb_attention(qkv, d, hp):
    b, s, _ = qkv.shape
    T = SB_TILE
    W = LANES * hp
    nblk = d // W
    kern = functools.partial(_sb_kernel, hp=hp)
    return pl.pallas_call(
        kern,
        grid=(b, nblk, s // T),
        in_specs=[pl.BlockSpec((1, T, W), lambda i, j, t: (i, t, j)),
                  pl.BlockSpec((1, s, W), lambda i, j, t: (i, 0, nblk + j)),
                  pl.BlockSpec((1, s, W), lambda i, j, t: (i, 0, 2 * nblk + j))],
        out_specs=pl.BlockSpec((1, T, W), lambda i, j, t: (i, t, j)),
        out_shape=jax.ShapeDtypeStruct((b, s, d), BF16),
        scratch_shapes=[pltpu.VMEM((2 * hp, T, LANES), F32), pltpu.VMEM((T, W), F32)],
        compiler_params=_params("parallel", "parallel", "arbitrary"),
        name="sb_attention",
    )(qkv, qkv, qkv)


def _sb_heads(x, gain, w_qkv, q_gain, k_gain):
    b, s, d = x.shape
    x2 = x.reshape(b * s, d)
    q_scale = math.log2(math.e) / math.sqrt(SB_HEAD_DIM)
    head_gain = jnp.concatenate([jnp.tile(q_gain * q_scale, SB_HEADS),
                                 jnp.tile(k_gain, SB_HEADS)]).reshape(1, 2 * d)
    qkv = _qkv_proj(x2, gain, w_qkv.astype(BF16), head_gain, 2 * d, TOKEN_TILE, COL_TILE)
    return _sb_attention(qkv.reshape(b, s, 3 * d), d, SB_HEAD_PAIRS).reshape(b * s, d)


def kernel(x, mix_norm, pool_in, pool_group, pool_scale, ssd_in, ssd_conv_w, ssd_conv_b, ssd_dt_bias,
           ssd_a_log, ssd_d, ssd_out_norm, ssd_out, sb_qkv, sb_q_norm, sb_k_norm, sb_out, ffn_norm,
           ffn_gate, ffn_up, ffn_down):
    b, s, d = x.shape
    depth = mix_norm.shape[0]
    for i in range(depth):
        kind, j = i % 3, i // 3
        mixer_out = w_proj = None
        if kind == 0:
            x = _pool_layer(x, mix_norm[i], pool_in[j].astype(BF16), pool_group[j].astype(BF16),
                            pool_scale[j], TOKEN_TILE)
        elif kind == 1:
            x = _ssd_layer(x, mix_norm[i], ssd_in[j], ssd_conv_w[j], ssd_conv_b[j], ssd_dt_bias[j],
                           ssd_a_log[j], ssd_d[j], ssd_out_norm[j], ssd_out[j])
        else:
            mixer_out = _sb_heads(x, mix_norm[i], sb_qkv[j], sb_q_norm[j], sb_k_norm[j])
            w_proj = sb_out[j].astype(BF16)
        x = _ffn(x.reshape(b * s, d), ffn_norm[i], ffn_gate[i].astype(BF16), ffn_up[i].astype(BF16),
                 ffn_down[i].astype(BF16), FFN_TOKEN_TILE, FFN_HIDDEN_TILE,
                 mixer_out, w_proj).reshape(b, s, d)
    return x
```
